```python
import math
import jax
import jax.numpy as jnp
from jax import lax
import numpy as np

D_MODEL = 2048
BATCH = 2
SEQ = 4096
DEPTH = 2
DEC_BATCH = 32
DEC_SEQ = 4
PAST_LEN = 8192
PAGE_SIZE = 128

HEAD_DIM = 64
ROPE_DIMS = HEAD_DIM // 4
ROPE_THETA = 500000.0
N_BRANCH = 4
BRANCH_WIDTH = D_MODEL // N_BRANCH
Q_BLOCK = 128
NORM_EPS = 1e-6
NEG_INF = -1e30
H_DIFF = BRANCH_WIDTH // (2 * HEAD_DIM)
H_FOX = BRANCH_WIDTH // HEAD_DIM
H_NSA = BRANCH_WIDTH // HEAD_DIM
G_NSA = 2
HPG_NSA = H_NSA // G_NSA
CMP_LEN = 32
CMP_STRIDE = 16
CMP_HIDDEN = 2 * HEAD_DIM
SEL_BLOCK = 64
N_SEL = 16
WINDOW = 512
SEL_BONUS = 1e4
H_MOBA = BRANCH_WIDTH // HEAD_DIM
MOBA_BLOCK = 256
MOBA_TOPK = 3
N_MEM = 256
H_MEM = 4
MEM_HEAD_DIM = BRANCH_WIDTH // H_MEM
N_GROUPS = 4
EXPERTS_PER_GROUP = 8
N_EXPERTS = N_GROUPS * EXPERTS_PER_GROUP
TOPK_IN_GROUP = 2
D_FF_EXPERT = D_MODEL // 4
MOE_BLOCK = 128
IN_SIZES = (
    2 * H_DIFF * HEAD_DIM, 2 * H_DIFF * HEAD_DIM, 2 * H_DIFF * HEAD_DIM,
    H_FOX * HEAD_DIM, H_FOX * HEAD_DIM, H_FOX * HEAD_DIM, H_FOX,
    H_NSA * HEAD_DIM, 6 * G_NSA * HEAD_DIM, 3 * H_NSA,
    H_MOBA * HEAD_DIM, H_MOBA * HEAD_DIM, H_MOBA * HEAD_DIM,
    N_BRANCH * D_MODEL,
)
D_IN = sum(IN_SIZES)

kernel_name = 'hybrid_gated_branch_decoder_step'


def _in_offsets():
    return [int(v) for v in np.cumsum(IN_SIZES)[:-1]]


def rms_norm(x, g):
    xf = x.astype(jnp.float32)
    y = xf * lax.rsqrt(jnp.mean(xf * xf, axis=-1, keepdims=True) + NORM_EPS)
    return (y * g.astype(jnp.float32)).astype(x.dtype)


def rope_partial(x, pos):
    half = ROPE_DIMS // 2
    inv_freq = jnp.exp(jnp.arange(half, dtype=jnp.float32) * (-2.0 * math.log(ROPE_THETA) / ROPE_DIMS))
    ang = jnp.asarray(pos, jnp.float32)[:, None] * inv_freq[None, :]
    cos = jnp.cos(ang)[None, :, None, :]
    sin = jnp.sin(ang)[None, :, None, :]
    xr = x[..., :ROPE_DIMS].astype(jnp.float32)
    x1, x2 = xr[..., :half], xr[..., half:]
    rot = jnp.concatenate([x1 * cos - x2 * sin, x2 * cos + x1 * sin], axis=-1).astype(x.dtype)
    return jnp.concatenate([rot, x[..., ROPE_DIMS:]], axis=-1)


def masked_softmax(s, mask):
    s = jnp.where(mask, s.astype(jnp.float32), NEG_INF)
    return jnp.where(mask, jax.nn.softmax(s, axis=-1), 0.0)


def causal_mask(q_pos, n_keys):
    return jnp.arange(n_keys, dtype=jnp.int32)[None, :] <= q_pos[:, None]


def gather_pages(pool, page_table):
    g = pool[page_table]
    return g.reshape((g.shape[0], g.shape[1] * g.shape[2]) + g.shape[3:])


def append_rows(past, new):
    return new if past is None else jnp.concatenate([past.astype(new.dtype), new], axis=1)


def to_blocks(k, block):
    B, T = k.shape[:2]
    n = -(-T // block)
    k = jnp.pad(k, ((0, 0), (0, n * block - T)) + ((0, 0),) * (k.ndim - 2))
    k = k.reshape((B, n, block) + k.shape[2:])
    return jnp.moveaxis(k, 3, 1)


def sweep_query_blocks(fn, q_pos, q):
    B, T = q.shape[:2]
    q_pos = jnp.asarray(q_pos, jnp.int32)
    if T <= Q_BLOCK or T % Q_BLOCK:
        return fn(q_pos, q)
    nb = T // Q_BLOCK

    def body(i):
        s = i * Q_BLOCK
        return fn(lax.dynamic_slice_in_dim(q_pos, s, Q_BLOCK), lax.dynamic_slice_in_dim(q, s, Q_BLOCK, axis=1))

    out = lax.map(body, jnp.arange(nb, dtype=jnp.int32))
    return jnp.moveaxis(out, 0, 1).reshape((B, T) + out.shape[3:])


def diff_attention(q, q_pos, k, v, lam, subln_g, lam_init):
    s = jnp.einsum('bqhcd,bkhcd->bchqk', q, k).astype(jnp.float32) * HEAD_DIM ** -0.5
    p = masked_softmax(s, causal_mask(q_pos, k.shape[1]))
    a = p[:, 0] - lam * p[:, 1]
    o = jnp.einsum('bhqk,bkhe->bqhe', a.astype(v.dtype), v)
    return rms_norm(o, subln_g) * (1.0 - lam_init)


def forgetting_attention(q, q_pos, k, v, c_all):
    s = jnp.einsum('bqhd,bkhd->bhqk', q, k).astype(jnp.float32) * HEAD_DIM ** -0.5
    c_q = jnp.take(c_all, q_pos, axis=1)
    s = s + jnp.transpose(c_q, (0, 2, 1))[..., None] - jnp.transpose(c_all, (0, 2, 1))[:, :, None, :]
    p = masked_softmax(s, causal_mask(q_pos, k.shape[1]))
    return jnp.einsum('bhqk,bkhd->bqhd', p.astype(v.dtype), v)


def compress_blocks(k, pe, w1, w2):
    B, Tk, G, _ = k.shape
    n_cmp = (Tk - CMP_LEN) // CMP_STRIDE + 1
    idx = np.arange(n_cmp)[:, None] * CMP_STRIDE + np.arange(CMP_LEN)[None, :]
    blk = k[:, idx] + pe[None, None, :, None, :].astype(k.dtype)
    blk = jnp.transpose(blk, (0, 1, 3, 2, 4)).reshape(B, n_cmp, G, CMP_LEN * HEAD_DIM)
    hid = jax.nn.gelu(jnp.einsum('bngi,if->bngf', blk, w1))
    return jnp.einsum('bngf,fd->bngd', hid, w2)


def overlap_matrix(n_cmp, n_slc):
    st = np.arange(n_cmp) * CMP_STRIDE
    en = st + CMP_LEN
    j = np.arange(n_slc) * SEL_BLOCK
    return ((st[:, None] < j[None, :] + SEL_BLOCK) & (en[:, None] > j[None, :])).astype(np.float32)


def nsa_compressed_selected(q, q_pos, k_cmp, v_cmp, ks_b, vs_b):
    B, Tq = q.shape[:2]
    n_cmp = k_cmp.shape[1]
    n_slc = ks_b.shape[2]
    scale = HEAD_DIM ** -0.5
    qg = q.reshape(B, Tq, G_NSA, HPG_NSA, HEAD_DIM)
    ends = jnp.arange(n_cmp, dtype=jnp.int32) * CMP_STRIDE + (CMP_LEN - 1)
    s_c = jnp.einsum('bqgmd,bngd->bgmqn', qg, k_cmp).astype(jnp.float32) * scale
    p_c = masked_softmax(s_c, ends[None, :] <= q_pos[:, None])
    o_cmp = jnp.einsum('bgmqn,bngd->bqgmd', p_c.astype(v_cmp.dtype), v_cmp)
    imp = jnp.einsum('bgmqn,ns->bgqs', p_c, jnp.asarray(overlap_matrix(n_cmp, n_slc)))
    own = q_pos // SEL_BLOCK
    j = jnp.arange(n_slc, dtype=jnp.int32)[None, :]
    valid = j <= own[:, None]
    forced = (j == 0) | (j == own[:, None]) | (j == own[:, None] - 1)
    score = jnp.where(valid, imp + jnp.where(forced, SEL_BONUS, 0.0), -SEL_BONUS)
    n_top = min(N_SEL, n_slc)
    _, idx = lax.top_k(score, n_top)
    ok = idx <= own[None, None, :, None]
    bi = jnp.arange(B)[:, None, None, None]
    gi = jnp.arange(G_NSA)[None, :, None, None]
    n_keys = n_top * SEL_BLOCK
    kg = ks_b[bi, gi, idx].reshape(B, G_NSA, Tq, n_keys, HEAD_DIM)
    vg = vs_b[bi, gi, idx].reshape(B, G_NSA, Tq, n_keys, HEAD_DIM)
    kpos = idx[..., None] * SEL_BLOCK + jnp.arange(SEL_BLOCK, dtype=jnp.int32)
    mask = (ok[..., None] & (kpos <= q_pos[None, None, :, None, None])).reshape(B, G_NSA, Tq, n_keys)
    s_s = jnp.einsum('bqgmd,bgqkd->bgmqk', qg, kg).astype(jnp.float32) * scale
    p_s = masked_softmax(s_s, mask[:, :, None])
    o_sel = jnp.einsum('bgmqk,bgqkd->bqgmd', p_s.astype(vg.dtype), vg)
    return jnp.stack([o_cmp, o_sel], axis=2).reshape(B, Tq, 2, H_NSA, HEAD_DIM)


def nsa_sliding_window(q, q_pos, win_new, win_buf):
    B, T = q.shape[:2]
    if win_buf is None:
        nb = T // Q_BLOCK
        kv_pad = jnp.pad(win_new, ((0, 0), (WINDOW, 0), (0, 0), (0, 0), (0, 0)))
        idx = np.arange(nb)[:, None] * Q_BLOCK + np.arange(Q_BLOCK + WINDOW)[None, :]
        kv = kv_pad[:, idx]
        kpos = idx - WINDOW
        qpos = np.asarray(q_pos).reshape(nb, Q_BLOCK)
        qg = q.reshape(B, nb, Q_BLOCK, G_NSA, HPG_NSA, HEAD_DIM)
        state = win_new[:, T - min(WINDOW, T):]
    else:
        wb = win_buf.shape[1]
        kv_all = jnp.concatenate([win_buf.astype(win_new.dtype), win_new], axis=1)
        kpos = (int(q_pos[0]) - wb + np.arange(wb + T))[None, :]
        qpos = np.asarray(q_pos)[None, :]
        qg = q.reshape(B, 1, T, G_NSA, HPG_NSA, HEAD_DIM)
        state = kv_all[:, T:]
        kv = kv_all[:, None]
    s = jnp.einsum('bnqgmd,bnkgd->bngmqk', qg, kv[:, :, :, 0]).astype(jnp.float32) * HEAD_DIM ** -0.5
    dist = qpos[:, :, None] - kpos[:, None, :]
    mask = (dist >= 0) & (dist <= WINDOW) & (kpos[:, None, :] >= 0)
    p = masked_softmax(s, jnp.asarray(mask)[None, :, None, None])
    o = jnp.einsum('bngmqk,bnkgd->bnqgmd', p.astype(kv.dtype), kv[:, :, :, 1])
    return o.reshape(B, T, H_NSA, HEAD_DIM), state


def moba_attention(q, q_pos, k_mean, k_b, v_b):
    B, Tq, H, _ = q.shape
    n_blk = k_mean.shape[2]
    own = q_pos // MOBA_BLOCK
    sc = jnp.einsum('bqhd,bhnd->bhqn', q.astype(jnp.float32), k_mean)
    past_blk = jnp.arange(n_blk, dtype=jnp.int32)[None, :] < own[:, None]
    sc = jnp.where(past_blk[None, None], sc, NEG_INF)
    n_top = min(MOBA_TOPK, n_blk)
    _, top = lax.top_k(sc, n_top)
    ok = top < own[None, None, :, None]
    idx = jnp.concatenate([top, jnp.broadcast_to(own[None, None, :, None], (B, H, Tq, 1)).astype(top.dtype)], axis=-1)
    ok = jnp.concatenate([ok, jnp.ones((B, H, Tq, 1), bool)], axis=-1)
    bi = jnp.arange(B)[:, None, None, None]
    hi = jnp.arange(H)[None, :, None, None]
    n_keys = (n_top + 1) * MOBA_BLOCK
    kg = k_b[bi, hi, idx].reshape(B, H, Tq, n_keys, HEAD_DIM)
    vg = v_b[bi, hi, idx].reshape(B, H, Tq, n_keys, HEAD_DIM)
    kpos = idx[..., None] * MOBA_BLOCK + jnp.arange(MOBA_BLOCK, dtype=jnp.int32)
    mask = (ok[..., None] & (kpos <= q_pos[None, None, :, None, None])).reshape(B, H, Tq, n_keys)
    s = jnp.einsum('bqhd,bhqkd->bhqk', q, kg).astype(jnp.float32) * HEAD_DIM ** -0.5
    p = masked_softmax(s, mask)
    return jnp.einsum('bhqk,bhqkd->bqhd', p.astype(vg.dtype), vg)


def token_mixers(h, q_pos, past, w_in, b_forget, diff_lambda, diff_subln_g, lam_init, nsa_pe, nsa_w1, nsa_w2, w_branch, w_out):
    B, T, _ = h.shape
    if past is None:
        p_diff = p_fox = p_logf = p_nsa = p_win = p_moba = None
    else:
        p_diff, p_fox, p_logf, p_nsa, p_win, p_moba = past
    proj = jnp.einsum('btd,de->bte', h, w_in)
    (dq, dk, dv, fq, fk, fv, fl, nq, nkv, ngate, mq, mk, mv, glog) = jnp.split(proj, _in_offsets(), axis=-1)

    dq = rope_partial(dq.reshape(B, T, 2 * H_DIFF, HEAD_DIM), q_pos).reshape(B, T, H_DIFF, 2, HEAD_DIM)
    dk = rope_partial(dk.reshape(B, T, 2 * H_DIFF, HEAD_DIM), q_pos).reshape(B, T, H_DIFF, 2 * HEAD_DIM)
    diff_new = jnp.stack([dk, dv.reshape(B, T, H_DIFF, 2 * HEAD_DIM)], axis=2)
    diff_all = append_rows(p_diff, diff_new)
    n_keys = diff_all.shape[1]
    k_a = diff_all[:, :, 0].reshape(B, n_keys, H_DIFF, 2, HEAD_DIM)
    v_a = diff_all[:, :, 1]
    lam_p = diff_lambda.astype(jnp.float32)
    lam = jnp.exp(jnp.sum(lam_p[0] * lam_p[1])) - jnp.exp(jnp.sum(lam_p[2] * lam_p[3])) + lam_init
    o_a = sweep_query_blocks(lambda qp, q: diff_attention(q, qp, k_a, v_a, lam, diff_subln_g, lam_init), q_pos, dq)

    fq = fq.reshape(B, T, H_FOX, HEAD_DIM)
    fox_new = jnp.stack([fk.reshape(B, T, H_FOX, HEAD_DIM), fv.reshape(B, T, H_FOX, HEAD_DIM)], axis=2)
    logf_new = jax.nn.log_sigmoid(fl.astype(jnp.float32) + b_forget.astype(jnp.float32))
    fox_all = append_rows(p_fox, fox_new)
    logf_all = append_rows(p_logf, logf_new)
    c_all = jnp.cumsum(logf_all.astype(jnp.float32), axis=1)
    o_b = sweep_query_blocks(lambda qp, q: forgetting_attention(q, qp, fox_all[:, :, 0], fox_all[:, :, 1], c_all), q_pos, fq)

    nq = rope_partial(nq.reshape(B, T, H_NSA, HEAD_DIM), q_pos)
    nkv = nkv.reshape(B, T, 6, G_NSA, HEAD_DIM)
    nsa_new = jnp.stack([nkv[:, :, 0], nkv[:, :, 1], rope_partial(nkv[:, :, 2], q_pos), nkv[:, :, 3]], axis=2)
    win_new = jnp.stack([rope_partial(nkv[:, :, 4], q_pos), nkv[:, :, 5]], axis=2)
    nsa_all = append_rows(p_nsa, nsa_new)
    k_cmp = compress_blocks(nsa_all[:, :, 0], nsa_pe[0], nsa_w1[0], nsa_w2[0])
    v_cmp = compress_blocks(nsa_all[:, :, 1], nsa_pe[1], nsa_w1[1], nsa_w2[1])
    ks_b = to_blocks(nsa_all[:, :, 2], SEL_BLOCK)
    vs_b = to_blocks(nsa_all[:, :, 3], SEL_BLOCK)
    o_cs = sweep_query_blocks(lambda qp, q: nsa_compressed_selected(q, qp, k_cmp, v_cmp, ks_b, vs_b), q_pos, nq)
    o_win, win_state = nsa_sliding_window(nq, q_pos, win_new, p_win)
    g = jax.nn.sigmoid(ngate.astype(jnp.float32)).reshape(B, T, H_NSA, 3)
    o_c = g[..., 0:1] * o_cs[:, :, 0] + g[..., 1:2] * o_cs[:, :, 1] + g[..., 2:3] * o_win

    mq = rope_partial(mq.reshape(B, T, H_MOBA, HEAD_DIM), q_pos)
    moba_new = jnp.stack([rope_partial(mk.reshape(B, T, H_MOBA, HEAD_DIM), q_pos), mv.reshape(B, T, H_MOBA, HEAD_DIM)], axis=2)
    moba_all = append_rows(p_moba, moba_new)
    mk_b = to_blocks(moba_all[:, :, 0], MOBA_BLOCK)
    mv_b = to_blocks(moba_all[:, :, 1], MOBA_BLOCK)
    k_mean = jnp.mean(mk_b.astype(jnp.float32), axis=3)
    o_d = sweep_query_blocks(lambda qp, q: moba_attention(q, qp, k_mean, mk_b, mv_b), q_pos, mq)

    branches = jnp.stack([o_a.reshape(B, T, BRANCH_WIDTH), o_b.reshape(B, T, BRANCH_WIDTH),
                          o_c.reshape(B, T, BRANCH_WIDTH), o_d.reshape(B, T, BRANCH_WIDTH)], axis=2).astype(h.dtype)
    br = jnp.einsum('btnw,nwd->btnd', branches, w_branch)
    gates = jax.nn.sigmoid(glog.reshape(B, T, N_BRANCH, D_MODEL).astype(jnp.float32))
    merged = jnp.sum(gates * br.astype(jnp.float32), axis=2).astype(h.dtype)
    y = jnp.einsum('btd,de->bte', merged, w_out)
    return y, diff_new, fox_new, logf_new, nsa_new, win_state, moba_new


def memory_kv(mem, g, w_kv):
    B, M, _ = mem.shape
    return jnp.einsum('bmd,de->bme', rms_norm(mem, g), w_kv).reshape(B, M, 2, H_MEM, MEM_HEAD_DIM)


def memory_attention(h, mem_kv, w_q, w_o):
    B, T, _ = h.shape
    q = jnp.einsum('btd,de->bte', h, w_q).reshape(B, T, H_MEM, MEM_HEAD_DIM)
    s = jnp.einsum('bqhd,bkhd->bhqk', q, mem_kv[:, :, 0].astype(q.dtype)).astype(jnp.float32) * MEM_HEAD_DIM ** -0.5
    p = jax.nn.softmax(s, axis=-1)
    o = jnp.einsum('bhqk,bkhd->bqhd', p.astype(h.dtype), mem_kv[:, :, 1].astype(h.dtype))
    return jnp.einsum('bqe,ed->bqd', o.reshape(B, T, H_MEM * MEM_HEAD_DIM), w_o)


def expert_dispatch(xt, eid, gate, w_g, w_u, w_d):
    N, D = xt.shape
    K = eid.shape[1]
    M = N * K
    flat_e = eid.reshape(M)
    order = jnp.argsort(flat_e)
    se = flat_e[order]
    counts = jnp.bincount(flat_e, length=N_EXPERTS).astype(jnp.int32)
    padded = ((counts + MOE_BLOCK - 1) // MOE_BLOCK) * MOE_BLOCK
    pad_end = jnp.cumsum(padded)
    pad_start = pad_end - padded
    start = jnp.cumsum(counts) - counts
    dest = pad_start[se] + (jnp.arange(M, dtype=jnp.int32) - start[se])
    n_blk = -(-(M + N_EXPERTS * (MOE_BLOCK - 1)) // MOE_BLOCK)
    tok = order // K
    xs = jnp.zeros((n_blk * MOE_BLOCK, D), xt.dtype).at[dest].set(xt[tok])
    blk_e = jnp.minimum(jnp.searchsorted(pad_end, jnp.arange(n_blk, dtype=jnp.int32) * MOE_BLOCK, side='right'), N_EXPERTS - 1)

    def run(args):
        xb, e = args
        hid = jax.nn.silu(xb @ w_g[e]) * (xb @ w_u[e])
        return hid @ w_d[e]

    ys = lax.map(run, (xs.reshape(n_blk, MOE_BLOCK, D), blk_e)).reshape(n_blk * MOE_BLOCK, D)
    contrib = ys[dest] * gate.reshape(M)[order][:, None].astype(ys.dtype)
    return jnp.zeros_like(xt).at[tok].add(contrib)


def hier_moe(h, w_rg, b_rg, w_re, b_re, w_g, w_u, w_d):
    B, T, D = h.shape
    xt = h.reshape(B * T, D)
    n = xt.shape[0]
    rows = jnp.arange(n)
    lg = jnp.einsum('nd,dg->ng', xt, w_rg).astype(jnp.float32) + b_rg.astype(jnp.float32)
    grp = jnp.argmax(lg, axis=-1).astype(jnp.int32)
    p_grp = jax.nn.softmax(lg, axis=-1)[rows, grp]
    le = (jnp.einsum('nd,de->ne', xt, w_re).astype(jnp.float32) + b_re.astype(jnp.float32)).reshape(n, N_GROUPS, EXPERTS_PER_GROUP)
    p_exp = jax.nn.softmax(le[rows, grp], axis=-1)
    w_top, i_top = lax.top_k(p_exp, TOPK_IN_GROUP)
    gate = p_grp[:, None] * w_top / jnp.sum(w_top, axis=-1, keepdims=True)
    eid = grp[:, None] * EXPERTS_PER_GROUP + i_top.astype(jnp.int32)
    return expert_dispatch(xt, eid, gate, w_g, w_u, w_d).reshape(B, T, D)


def setup_inputs(seed: int = 0) -> dict:
    key = jax.random.key(seed)
    ks = jax.random.split(key, 40)
    f32 = jnp.float32
    n_pages = PAST_LEN // PAGE_SIZE
    n_pool = (DEC_BATCH * n_pages * 5) // 4
    win_buf = min(WINDOW, PAST_LEN)

    def nrm(k, shape, scale=1.0):
        return jax.random.normal(k, shape, f32) * scale

    def gain(k, shape):
        return 1.0 + 0.02 * jax.random.normal(k, shape, f32)

    page_table = jax.random.permutation(ks[0], n_pool)[:DEC_BATCH * n_pages].reshape(DEC_BATCH, n_pages).astype(jnp.int32)
    return {
        'x_prompt': nrm(ks[1], (BATCH, SEQ, D_MODEL)),
        'x_sample': nrm(ks[2], (DEC_BATCH, DEC_SEQ, D_MODEL)),
        'cache_diff_kv': nrm(ks[3], (DEPTH, n_pool, PAGE_SIZE, 2, H_DIFF, 2 * HEAD_DIM)),
        'cache_fox_kv': nrm(ks[4], (DEPTH, n_pool, PAGE_SIZE, 2, H_FOX, HEAD_DIM)),
        'cache_fox_logf': jax.nn.log_sigmoid(nrm(ks[5], (DEPTH, n_pool, PAGE_SIZE, H_FOX)) + 3.0),
        'cache_nsa_kv': nrm(ks[6], (DEPTH, n_pool, PAGE_SIZE, 4, G_NSA, HEAD_DIM)),
        'cache_nsa_win': nrm(ks[7], (DEPTH, DEC_BATCH, win_buf, 2, G_NSA, HEAD_DIM)),
        'cache_moba_kv': nrm(ks[8], (DEPTH, n_pool, PAGE_SIZE, 2, H_MOBA, HEAD_DIM)),
        'cache_mem_kv': nrm(ks[9], (DEPTH, DEC_BATCH, N_MEM, 2, H_MEM, MEM_HEAD_DIM)),
        'page_table': page_table,
        'mem_prompt': nrm(ks[10], (BATCH, N_MEM, D_MODEL)),
        'norm_mix_g': gain(ks[11], (DEPTH, D_MODEL)),
        'w_in': nrm(ks[12], (DEPTH, D_MODEL, D_IN), D_MODEL ** -0.5),
        'b_forget': jax.random.uniform(ks[13], (DEPTH, H_FOX), f32, 1.0, 6.0),
        'diff_lambda': nrm(ks[14], (DEPTH, 4, HEAD_DIM), 0.1),
        'diff_subln_g': gain(ks[15], (DEPTH, 2 * HEAD_DIM)),
        'nsa_cmp_pe': nrm(ks[16], (DEPTH, 2, CMP_LEN, HEAD_DIM), 0.02),
        'nsa_cmp_w1': nrm(ks[17], (DEPTH, 2, CMP_LEN * HEAD_DIM, CMP_HIDDEN), (CMP_LEN * HEAD_DIM) ** -0.5),
        'nsa_cmp_w2': nrm(ks[18], (DEPTH, 2, CMP_HIDDEN, HEAD_DIM), CMP_HIDDEN ** -0.5),
        'w_branch': nrm(ks[19], (DEPTH, N_BRANCH, BRANCH_WIDTH, D_MODEL), BRANCH_WIDTH ** -0.5),
        'w_out': nrm(ks[20], (DEPTH, D_MODEL, D_MODEL), D_MODEL ** -0.5),
        'norm_mem_g': gain(ks[21], (DEPTH, D_MODEL)),
        'norm_memkv_g': gain(ks[22], (DEPTH, D_MODEL)),
        'w_mem_q': nrm(ks[23], (DEPTH, D_MODEL, H_MEM * MEM_HEAD_DIM), D_MODEL ** -0.5),
        'w_mem_kv': nrm(ks[24], (DEPTH, D_MODEL, 2 * H_MEM * MEM_HEAD_DIM), D_MODEL ** -0.5),
        'w_mem_o': nrm(ks[25], (DEPTH, H_MEM * MEM_HEAD_DIM, D_MODEL), (H_MEM * MEM_HEAD_DIM) ** -0.5),
        'norm_ffn_g': gain(ks[26], (DEPTH, D_MODEL)),
        'w_router_group': nrm(ks[27], (DEPTH, D_MODEL, N_GROUPS), D_MODEL ** -0.5),
        'b_router_group': nrm(ks[28], (DEPTH, N_GROUPS), 0.01),
        'w_router_expert': nrm(ks[29], (DEPTH, D_MODEL, N_EXPERTS), D_MODEL ** -0.5),
        'b_router_expert': nrm(ks[30], (DEPTH, N_EXPERTS), 0.01),
        'w_exp_gate': nrm(ks[31], (DEPTH, N_EXPERTS, D_MODEL, D_FF_EXPERT), D_MODEL ** -0.5),
        'w_exp_up': nrm(ks[32], (DEPTH, N_EXPERTS, D_MODEL, D_FF_EXPERT), D_MODEL ** -0.5),
        'w_exp_down': nrm(ks[33], (DEPTH, N_EXPERTS, D_FF_EXPERT, D_MODEL), D_FF_EXPERT ** -0.5),
        'norm_final_g': gain(ks[34], (D_MODEL,)),
    }


def reference(x_prompt, x_sample, cache_diff_kv, cache_fox_kv, cache_fox_logf, cache_nsa_kv, cache_nsa_win,
              cache_moba_kv, cache_mem_kv, page_table, mem_prompt, norm_mix_g, w_in, b_forget, diff_lambda,
              diff_subln_g, nsa_cmp_pe, nsa_cmp_w1, nsa_cmp_w2, w_branch, w_out, norm_mem_g, norm_memkv_g,
              w_mem_q, w_mem_kv, w_mem_o, norm_ffn_g, w_router_group, b_router_group, w_router_expert,
              b_router_expert, w_exp_gate, w_exp_up, w_exp_down, norm_final_g):
    past_len = int(page_table.shape[1] * cache_diff_kv.shape[2])
    pos_p = np.arange(x_prompt.shape[1], dtype=np.int32)
    pos_s = past_len + np.arange(x_sample.shape[1], dtype=np.int32)
    xp, xs = x_prompt, x_sample
    acc_p = [[] for _ in range(7)]
    acc_s = [[] for _ in range(6)]
    for l in range(DEPTH):
        lam_init = 0.8 - 0.6 * math.exp(-0.3 * l)
        mix_w = (w_in[l], b_forget[l], diff_lambda[l], diff_subln_g[l], lam_init,
                 nsa_cmp_pe[l], nsa_cmp_w1[l], nsa_cmp_w2[l], w_branch[l], w_out[l])
        moe_w = (w_router_group[l], b_router_group[l], w_router_expert[l], b_router_expert[l],
                 w_exp_gate[l], w_exp_up[l], w_exp_down[l])
        y, dkv, fkv, flf, nkv, nwin, mkv = token_mixers(rms_norm(xp, norm_mix_g[l]), pos_p, None, *mix_w)
        xp = xp + y
        mem_kv_p = memory_kv(mem_prompt, norm_memkv_g[l], w_mem_kv[l])
        xp = xp + memory_attention(rms_norm(xp, norm_mem_g[l]), mem_kv_p, w_mem_q[l], w_mem_o[l])
        xp = xp + hier_moe(rms_norm(xp, norm_ffn_g[l]), *moe_w)
        for lst, val in zip(acc_p, (dkv, fkv, flf, nkv, nwin, mkv, mem_kv_p)):
            lst.append(val)
        past = (gather_pages(cache_diff_kv[l], page_table), gather_pages(cache_fox_kv[l], page_table),
                gather_pages(cache_fox_logf[l], page_table), gather_pages(cache_nsa_kv[l], page_table),
                cache_nsa_win[l], gather_pages(cache_moba_kv[l], page_table))
        y, dkv, fkv, flf, nkv, nwin, mkv = token_mixers(rms_norm(xs, norm_mix_g[l]), pos_s, past, *mix_w)
        xs = xs + y
        xs = xs + memory_attention(rms_norm(xs, norm_mem_g[l]), cache_mem_kv[l], w_mem_q[l], w_mem_o[l])
        xs = xs + hier_moe(rms_norm(xs, norm_ffn_g[l]), *moe_w)
        for lst, val in zip(acc_s, (dkv, fkv, flf, nkv, nwin, mkv)):
            lst.append(val)
    y_prompt = rms_norm(xp, norm_final_g)
    y_sample = rms_norm(xs, norm_final_g)
    new_diff_kv_prompt = jnp.stack(acc_p[0])
    new_fox_kv_prompt = jnp.stack(acc_p[1])
    new_fox_logf_prompt = jnp.stack(acc_p[2])
    new_nsa_kv_prompt = jnp.stack(acc_p[3])
    new_nsa_win_prompt = jnp.stack(acc_p[4])
    new_moba_kv_prompt = jnp.stack(acc_p[5])
    new_mem_kv_prompt = jnp.stack(acc_p[6])
    new_diff_kv_sample = jnp.stack(acc_s[0])
    new_fox_kv_sample = jnp.stack(acc_s[1])
    new_fox_logf_sample = jnp.stack(acc_s[2])
    new_nsa_kv_sample = jnp.stack(acc_s[3])
    new_nsa_win_sample = jnp.stack(acc_s[4])
    new_moba_kv_sample = jnp.stack(acc_s[5])
    return (y_prompt, y_sample,
            new_diff_kv_prompt, new_fox_kv_prompt, new_fox_logf_prompt, new_nsa_kv_prompt,
            new_nsa_win_prompt, new_moba_kv_prompt, new_mem_kv_prompt,
            new_diff_kv_sample, new_fox_kv_sample, new_fox_logf_sample, new_nsa_kv_sample,
            new_nsa_win_sample, new_moba_kv_sample)
```

```python
import functools
import math

import numpy as np
import jax
import jax.numpy as jnp
from jax import lax
from jax.experimental import pallas as pl
from jax.experimental.pallas import tpu as pltpu

F32 = jnp.float32
BF = jnp.bfloat16

HEAD_DIM = 64
ROPE_DIMS = HEAD_DIM // 4
ROPE_THETA = 500000.0
N_BRANCH = 4
NORM_EPS = 1e-6
NEG_INF = -1e30
G_NSA = 2
CMP_LEN = 32
CMP_STRIDE = 16
SEL_BLOCK = 64
N_SEL = 16
WINDOW = 512
SEL_BONUS = 1e4
MOBA_BLOCK = 256
MOBA_TOPK = 3
H_MEM = 4
N_GROUPS = 4
EXPERTS_PER_GROUP = 8
TOPK_IN_GROUP = 2
MOE_BLOCK = 128
LANES = 128
NEW_PAD = 128

VMEM_LIMIT = 56 * 1024 * 1024


def _cparams(*sem):
    return pltpu.CompilerParams(dimension_semantics=sem, vmem_limit_bytes=VMEM_LIMIT)


def _tile(n, target, mult=8):
    best = None
    for t in range(mult, min(n, target) + 1, mult):
        if n % t == 0:
            best = t
    return best if best is not None else n


def _rms_kernel(x_ref, g_ref, o_ref):
    x = x_ref[...].astype(F32)
    r = lax.rsqrt(jnp.mean(x * x, axis=-1, keepdims=True) + NORM_EPS)
    o_ref[...] = ((x * r) * g_ref[...]).astype(o_ref.dtype)


def rmsnorm(x, g, dtype):
    n, d = x.shape
    tm = _tile(n, 512)
    return pl.pallas_call(
        _rms_kernel,
        grid=(n // tm,),
        in_specs=[pl.BlockSpec((tm, d), lambda i: (i, 0)), pl.BlockSpec((1, d), lambda i: (0, 0))],
        out_specs=pl.BlockSpec((tm, d), lambda i: (i, 0)),
        out_shape=jax.ShapeDtypeStruct((n, d), dtype),
        compiler_params=_cparams("parallel"),
        name="rmsnorm",
    )(x, g.reshape(1, d).astype(F32))


def _mm_kernel(a_ref, w_ref, o_ref):
    o_ref[...] = jnp.dot(a_ref[...].astype(BF), w_ref[...].astype(BF),
                         preferred_element_type=F32).astype(o_ref.dtype)


def _mm_res_kernel(a_ref, w_ref, r_ref, o_ref):
    y = jnp.dot(a_ref[...].astype(BF), w_ref[...].astype(BF), preferred_element_type=F32)
    o_ref[...] = (r_ref[...] + y).astype(o_ref.dtype)


def matmul(a, w, res=None, out_dtype=F32, tm_target=640, tn_target=512):
    m, k = a.shape
    n = w.shape[1]
    tm = _tile(m, tm_target)
    tn = _tile(n, tn_target, LANES)
    in_specs = [pl.BlockSpec((tm, k), lambda i, j: (i, 0)), pl.BlockSpec((k, tn), lambda i, j: (0, j))]
    args = [a, w]
    kern = _mm_kernel
    if res is not None:
        in_specs.append(pl.BlockSpec((tm, tn), lambda i, j: (i, j)))
        args.append(res)
        kern = _mm_res_kernel
    return pl.pallas_call(
        kern,
        grid=(m // tm, n // tn),
        in_specs=in_specs,
        out_specs=pl.BlockSpec((tm, tn), lambda i, j: (i, j)),
        out_shape=jax.ShapeDtypeStruct((m, n), out_dtype),
        compiler_params=_cparams("parallel", "arbitrary"),
        name="matmul",
    )(*args)


def _cmp_mlp_kernel(x_ref, w1_ref, w2_ref, o_ref):
    h = jnp.dot(x_ref[...].astype(BF), w1_ref[...].astype(BF), preferred_element_type=F32)
    h = jax.nn.gelu(h)
    o_ref[...] = jnp.dot(h.astype(BF), w2_ref[...].astype(BF), preferred_element_type=F32)


def cmp_mlp(x, w1, w2):
    r, k = x.shape
    f = w1.shape[1]
    d = w2.shape[1]
    tm = _tile(r, 512)
    return pl.pallas_call(
        _cmp_mlp_kernel,
        grid=(r // tm,),
        in_specs=[pl.BlockSpec((tm, k), lambda i: (i, 0)), pl.BlockSpec((k, f), lambda i: (0, 0)),
                  pl.BlockSpec((f, d), lambda i: (0, 0))],
        out_specs=pl.BlockSpec((tm, d), lambda i: (i, 0)),
        out_shape=jax.ShapeDtypeStruct((r, d), F32),
        compiler_params=_cparams("parallel"),
        name="cmp_mlp",
    )(x, w1, w2)


def _flash_kernel(*refs, mode, tq, tk, n_k, scale, causal, window, bias, sel, lam_init):
    it = iter(refs)
    q_ref, k_ref, v_ref = next(it), next(it), next(it)
    cq_ref = ck_ref = sel_ref = e_ref = lam_ref = g_ref = None
    if bias:
        cq_ref, ck_ref = next(it), next(it)
    if sel:
        sel_ref, e_ref = next(it), next(it)
    if mode == "diff":
        lam_ref, g_ref = next(it), next(it)
    o_ref, m_ref, l_ref, acc_ref = next(it), next(it), next(it), next(it)

    n_sub = 1 if mode == "full" else 2
    qi = pl.program_id(2)
    lane = lax.broadcasted_iota(jnp.int32, (1, LANES), 1)
    q = q_ref[...].astype(F32) * scale
    if n_sub == 2:
        qs = [jnp.where(lane < HEAD_DIM, q, 0.0).astype(BF), jnp.where(lane >= HEAD_DIM, q, 0.0).astype(BF)]
    else:
        qs = [q.astype(BF)]
    qpos = qi * tq + lax.broadcasted_iota(jnp.int32, (tq, 1), 0)

    m_ref[...] = jnp.full(m_ref.shape, NEG_INF, F32)
    l_ref[...] = jnp.zeros(l_ref.shape, F32)
    acc_ref[...] = jnp.zeros(acc_ref.shape, F32)

    if sel:
        sels = [sel_ref[e if sel == "per_head" else 0] for e in range(n_sub)]
    if bias:
        cqs = [cq_ref[:, e:e + 1] for e in range(n_sub)]

    def body(j, carry):
        off = pl.multiple_of(j * tk, tk)
        kb = k_ref[pl.ds(off, tk), :].astype(BF)
        vb = v_ref[pl.ds(off, tk), :].astype(BF)
        mask = None
        if causal:
            kpos = off + lax.broadcasted_iota(jnp.int32, (1, tk), 1)
            mask = kpos <= qpos
            if window:
                mask = jnp.logical_and(mask, qpos - kpos <= WINDOW)
        for e in range(n_sub):
            s = lax.dot_general(qs[e], kb, (((1,), (1,)), ((), ())), preferred_element_type=F32)
            if bias:
                s = s + (cqs[e] - ck_ref[e:e + 1, pl.ds(off, tk)])
            me = mask
            if sel:
                selm = jnp.dot(sels[e], e_ref[:, pl.ds(off, tk)], preferred_element_type=F32)
                me = jnp.logical_and(selm > 0.5, mask)
            if me is not None:
                s = jnp.where(me, s, NEG_INF)
            m_prev = m_ref[e]
            m_new = jnp.maximum(m_prev, jnp.max(s, axis=-1, keepdims=True))
            alpha = jnp.exp(m_prev - m_new)
            p = jnp.exp(s - m_new)
            if me is not None:
                p = jnp.where(me, p, 0.0)
            l_ref[e] = alpha * l_ref[e] + jnp.sum(p, axis=-1, keepdims=True)
            acc_ref[e] = alpha * acc_ref[e] + jnp.dot(p.astype(BF), vb, preferred_element_type=F32)
            m_ref[e] = m_new
        return carry

    if causal:
        hi = ((qi + 1) * tq + tk - 1) // tk
        lo = jnp.maximum(qi * tq - WINDOW, 0) // tk if window else 0
    else:
        lo, hi = 0, n_k
    lax.fori_loop(lo, hi, body, 0)

    outs = []
    for e in range(n_sub):
        l = l_ref[e]
        inv = jnp.where(l > 0.0, 1.0 / l, 0.0)
        outs.append(acc_ref[e] * inv)
    if mode == "full":
        o = outs[0]
    elif mode == "pair":
        o = jnp.where(lane < HEAD_DIM, outs[0], outs[1])
    else:
        o = outs[0] - lam_ref[0] * outs[1]
        y = o * lax.rsqrt(jnp.mean(o * o, axis=-1, keepdims=True) + NORM_EPS)
        o = (y * g_ref[...]) * (1.0 - lam_init)
    o_ref[...] = o.astype(o_ref.dtype)


def flash_prompt(q, k, v, *, n_batch, t_q, t_k, n_cols, q_col, k_col, v_col, mode, scale,
                 causal=True, window=False, bias=None, sel=None, sel_mode=None, emat=None,
                 lam=None, subln_g=None, lam_init=0.0, tq=256, tk=256):
    tq = min(tq, t_q)
    tk = min(tk, t_k)
    nq = t_q // tq
    n_k = t_k // tk
    in_specs = [
        pl.BlockSpec((tq, LANES), lambda b, c, i: (b * nq + i, q_col(c))),
        pl.BlockSpec((t_k, LANES), lambda b, c, i: (b, k_col(c))),
        pl.BlockSpec((t_k, LANES), lambda b, c, i: (b, v_col(c))),
    ]
    args = [q, k, v]
    if bias is not None:
        cq, ck = bias
        in_specs += [pl.BlockSpec((None, None, tq, 2), lambda b, c, i: (b, c, i, 0)),
                     pl.BlockSpec((None, None, 2, t_k), lambda b, c, i: (b, c, 0, 0))]
        args += [cq, ck]
    if sel is not None:
        if sel_mode == "per_head":
            in_specs.append(pl.BlockSpec((None, 2, tq, LANES), lambda b, c, i: (b, c, i, 0)))
        else:
            in_specs.append(pl.BlockSpec((None, 1, tq, LANES), lambda b, c, i: (b, c // 2, i, 0)))
        in_specs.append(pl.BlockSpec((LANES, t_k), lambda b, c, i: (0, 0)))
        args += [sel, emat]
    if mode == "diff":
        in_specs += [pl.BlockSpec(memory_space=pltpu.SMEM), pl.BlockSpec((1, LANES), lambda b, c, i: (0, 0))]
        args += [lam.reshape(1).astype(F32), subln_g.reshape(1, LANES).astype(F32)]
    n_sub = 1 if mode == "full" else 2
    kern = functools.partial(_flash_kernel, mode=mode, tq=tq, tk=tk, n_k=n_k, scale=scale, causal=causal,
                             window=window, bias=bias is not None, sel=sel_mode if sel is not None else None,
                             lam_init=lam_init)
    return pl.pallas_call(
        kern,
        grid=(n_batch, n_cols, nq),
        in_specs=in_specs,
        out_specs=pl.BlockSpec((tq, LANES), lambda b, c, i: (b * nq + i, c)),
        out_shape=jax.ShapeDtypeStruct((n_batch * t_q, n_cols * LANES), F32),
        scratch_shapes=[pltpu.VMEM((n_sub, tq, 1), F32), pltpu.VMEM((n_sub, tq, 1), F32),
                        pltpu.VMEM((n_sub, tq, LANES), F32)],
        compiler_params=_cparams("parallel", "parallel", "arbitrary"),
        name="flash_" + mode,
    )(*args)


def _cmp_attn_kernel(q_ref, k_ref, v_ref, ov_ref, o_ref, imp_ref, *, tq, scale):
    qi = pl.program_id(2)
    lane = lax.broadcasted_iota(jnp.int32, (1, LANES), 1)
    n_c = k_ref.shape[0]
    qpos = qi * tq + lax.broadcasted_iota(jnp.int32, (tq, 1), 0)
    ends = lax.broadcasted_iota(jnp.int32, (1, n_c), 1) * CMP_STRIDE + (CMP_LEN - 1)
    mask = ends <= qpos
    kb = k_ref[...].astype(BF)
    vb = v_ref[...].astype(BF)
    ov = ov_ref[...]
    imp = jnp.zeros((tq, LANES), F32)
    halves = []
    for pair in range(2):
        q = q_ref[:, pair * LANES:(pair + 1) * LANES].astype(F32) * scale
        outs = []
        for e in range(2):
            qe = (jnp.where(lane < HEAD_DIM, q, 0.0) if e == 0 else jnp.where(lane >= HEAD_DIM, q, 0.0)).astype(BF)
            s = lax.dot_general(qe, kb, (((1,), (1,)), ((), ())), preferred_element_type=F32)
            s = jnp.where(mask, s, NEG_INF)
            m = jnp.max(s, axis=-1, keepdims=True)
            p = jnp.where(mask, jnp.exp(s - m), 0.0)
            l = jnp.sum(p, axis=-1, keepdims=True)
            p = p * jnp.where(l > 0.0, 1.0 / l, 0.0)
            pb = p.astype(BF)
            outs.append(jnp.dot(pb, vb, preferred_element_type=F32))
            imp = imp + jnp.dot(pb, ov, preferred_element_type=F32)
        halves.append(jnp.where(lane < HEAD_DIM, outs[0], outs[1]))
    o_ref[:, 0:LANES] = halves[0]
    o_ref[:, LANES:2 * LANES] = halves[1]
    imp_ref[...] = imp


def cmp_attn_prompt(q, k_dup, v_dup, ov, *, n_batch, t_q, tq=256):
    tq = min(tq, t_q)
    nq = t_q // tq
    n_c = k_dup.shape[2]
    kern = functools.partial(_cmp_attn_kernel, tq=tq, scale=HEAD_DIM ** -0.5)
    return pl.pallas_call(
        kern,
        grid=(n_batch, G_NSA, nq),
        in_specs=[pl.BlockSpec((tq, 2 * LANES), lambda b, g, i: (b * nq + i, g)),
                  pl.BlockSpec((None, None, n_c, LANES), lambda b, g, i: (b, g, 0, 0)),
                  pl.BlockSpec((None, None, n_c, LANES), lambda b, g, i: (b, g, 0, 0)),
                  pl.BlockSpec((n_c, LANES), lambda b, g, i: (0, 0))],
        out_specs=[pl.BlockSpec((tq, 2 * LANES), lambda b, g, i: (b * nq + i, g)),
                   pl.BlockSpec((None, None, tq, LANES), lambda b, g, i: (b, g, i, 0))],
        out_shape=[jax.ShapeDtypeStruct((n_batch * t_q, G_NSA * 2 * LANES), F32),
                   jax.ShapeDtypeStruct((n_batch, G_NSA, t_q, LANES), F32)],
        compiler_params=_cparams("parallel", "parallel", "arbitrary"),
        name="cmp_attn",
    )(q, k_dup, v_dup, ov)


def _decode_kernel(*refs, has_bias, has_new, emit_p):
    it = iter(refs)
    q_ref, k_ref, v_ref = next(it), next(it), next(it)
    b_ref = next(it) if has_bias else None
    if has_new:
        kn_ref, vn_ref, bn_ref = next(it), next(it), next(it)
    o_ref = next(it)
    p_ref = next(it) if emit_p else None
    m_ref, l_ref, acc_ref = next(it), next(it), next(it)
    j = pl.program_id(1)
    last = pl.num_programs(1) - 1

    @pl.when(j == 0)
    def _():
        m_ref[...] = jnp.full(m_ref.shape, NEG_INF, F32)
        l_ref[...] = jnp.zeros(l_ref.shape, F32)
        acc_ref[...] = jnp.zeros(acc_ref.shape, F32)

    q = q_ref[...]

    def update(kb, vb, b):
        s = lax.dot_general(q, kb.astype(BF), (((1,), (1,)), ((), ())), preferred_element_type=F32)
        if b is not None:
            s = s + b
        m_prev = m_ref[...]
        m_new = jnp.maximum(m_prev, jnp.max(s, axis=-1, keepdims=True))
        alpha = jnp.exp(m_prev - m_new)
        p = jnp.exp(s - m_new)
        l_ref[...] = alpha * l_ref[...] + jnp.sum(p, axis=-1, keepdims=True)
        acc_ref[...] = alpha * acc_ref[...] + jnp.dot(p.astype(BF), vb.astype(BF), preferred_element_type=F32)
        m_ref[...] = m_new
        return p

    p = update(k_ref[...], v_ref[...], b_ref[...] if has_bias else None)

    @pl.when(j == last)
    def _():
        if has_new:
            update(kn_ref[...], vn_ref[...], bn_ref[...])
        inv = 1.0 / l_ref[...]
        o_ref[...] = acc_ref[...] * inv
        if emit_p:
            p_ref[...] = p * inv


def decode_attn(qbd, k, v, *, k_col, v_col, ck, cv, bias=None, new=None, emit_p=False, rk=1024):
    n_b, r = k.shape[0], k.shape[1]
    rk = min(rk, r)
    n_j = r // rk
    assert not emit_p or (n_j == 1 and new is None)
    in_specs = [pl.BlockSpec((None, LANES, ck), lambda b, j: (b, 0, 0)),
                pl.BlockSpec((None, rk, ck), lambda b, j: (b, j, k_col)),
                pl.BlockSpec((None, rk, cv), lambda b, j: (b, j, v_col))]
    args = [qbd, k, v]
    if bias is not None:
        in_specs.append(pl.BlockSpec((None, LANES, rk), lambda b, j: (b, 0, j)))
        args.append(bias)
    if new is not None:
        kn, vn, bn = new
        in_specs += [pl.BlockSpec((None, NEW_PAD, ck), lambda b, j: (b, 0, k_col)),
                     pl.BlockSpec((None, NEW_PAD, cv), lambda b, j: (b, 0, v_col)),
                     pl.BlockSpec((None, LANES, NEW_PAD), lambda b, j: (b, 0, 0))]
        args += [kn, vn, bn]
    out_specs = [pl.BlockSpec((None, LANES, cv), lambda b, j: (b, 0, 0))]
    out_shape = [jax.ShapeDtypeStruct((n_b, LANES, cv), F32)]
    if emit_p:
        out_specs.append(pl.BlockSpec((None, LANES, rk), lambda b, j: (b, 0, 0)))
        out_shape.append(jax.ShapeDtypeStruct((n_b, LANES, r), F32))
    kern = functools.partial(_decode_kernel, has_bias=bias is not None, has_new=new is not None, emit_p=emit_p)
    out = pl.pallas_call(
        kern,
        grid=(n_b, n_j),
        in_specs=in_specs,
        out_specs=out_specs,
        out_shape=out_shape,
        scratch_shapes=[pltpu.VMEM((LANES, 1), F32), pltpu.VMEM((LANES, 1), F32), pltpu.VMEM((LANES, cv), F32)],
        compiler_params=_cparams("parallel", "arbitrary"),
        name="decode_attn",
    )(*args)
    return out if emit_p else out[0]


def _merge_kernel(a_ref, b_ref, c_ref, d_ref, w_ref, g0_ref, g1_ref, g2_ref, g3_ref, o_ref):
    acc = None
    for n, (br, g) in enumerate(((a_ref, g0_ref), (b_ref, g1_ref), (c_ref, g2_ref), (d_ref, g3_ref))):
        y = jnp.dot(br[...].astype(BF), w_ref[n].astype(BF), preferred_element_type=F32)
        t = jax.nn.sigmoid(g[...].astype(F32)) * y
        acc = t if acc is None else acc + t
    o_ref[...] = acc.astype(o_ref.dtype)


def merge_branches(branches, w_branch, glog, d_model):
    n, bw = branches[0].shape
    tm = _tile(n, 640)
    tn = _tile(d_model, 512, LANES)
    nj = d_model // tn
    in_specs = [pl.BlockSpec((tm, bw), lambda i, j: (i, 0)) for _ in range(N_BRANCH)]
    in_specs.append(pl.BlockSpec((N_BRANCH, bw, tn), lambda i, j: (0, 0, j)))
    for nb in range(N_BRANCH):
        in_specs.append(pl.BlockSpec((tm, tn), functools.partial(lambda i, j, nb: (i, nb * nj + j), nb=nb)))
    return pl.pallas_call(
        _merge_kernel,
        grid=(n // tm, nj),
        in_specs=in_specs,
        out_specs=pl.BlockSpec((tm, tn), lambda i, j: (i, j)),
        out_shape=jax.ShapeDtypeStruct((n, d_model), BF),
        compiler_params=_cparams("parallel", "arbitrary"),
        name="merge",
    )(*branches, w_branch, glog, glog, glog, glog)


def _moe_kernel(be_ref, x_ref, wg_ref, wu_ref, wd_ref, o_ref):
    x = x_ref[...].astype(BF)
    g = jnp.dot(x, wg_ref[...].astype(BF), preferred_element_type=F32)
    u = jnp.dot(x, wu_ref[...].astype(BF), preferred_element_type=F32)
    hid = (jax.nn.silu(g) * u).astype(BF)
    o_ref[...] = jnp.dot(hid, wd_ref[...].astype(BF), preferred_element_type=F32)


def moe_experts(xs, blk_e, w_g, w_u, w_d):
    rows, d = xs.shape
    f = w_g.shape[2]
    n_blk = rows // MOE_BLOCK
    grid_spec = pltpu.PrefetchScalarGridSpec(
        num_scalar_prefetch=1,
        grid=(n_blk,),
        in_specs=[pl.BlockSpec((MOE_BLOCK, d), lambda i, be: (i, 0)),
                  pl.BlockSpec((None, d, f), lambda i, be: (be[i], 0, 0)),
                  pl.BlockSpec((None, d, f), lambda i, be: (be[i], 0, 0)),
                  pl.BlockSpec((None, f, d), lambda i, be: (be[i], 0, 0))],
        out_specs=pl.BlockSpec((MOE_BLOCK, d), lambda i, be: (i, 0)),
    )
    return pl.pallas_call(
        _moe_kernel,
        grid_spec=grid_spec,
        out_shape=jax.ShapeDtypeStruct((rows, d), F32),
        compiler_params=_cparams("arbitrary"),
        name="moe_experts",
    )(blk_e, xs, w_g, w_u, w_d)


def _rope_tables(pos):
    half = ROPE_DIMS // 2
    inv_freq = jnp.exp(jnp.arange(half, dtype=F32) * (-2.0 * math.log(ROPE_THETA) / ROPE_DIMS))
    ang = jnp.asarray(pos, F32)[:, None] * inv_freq[None, :]
    return jnp.cos(ang), jnp.sin(ang)


def _rope(x, cos, sin):
    half = ROPE_DIMS // 2
    c = cos[:, None, :]
    s = sin[:, None, :]
    x1, x2 = x[..., :half], x[..., half:ROPE_DIMS]
    return jnp.concatenate([x1 * c - x2 * s, x2 * c + x1 * s, x[..., ROPE_DIMS:]], axis=-1)


def _overlap_matrix(n_cmp, n_slc):
    st = np.arange(n_cmp) * CMP_STRIDE
    en = st + CMP_LEN
    j = np.arange(n_slc) * SEL_BLOCK
    return ((st[:, None] < j[None, :] + SEL_BLOCK) & (en[:, None] > j[None, :])).astype(np.float32)


def _block_expand(n_rows, block, n_cols):
    c = np.arange(n_cols) // block
    return (np.arange(n_rows)[:, None] == c[None, :]).astype(np.float32)


def _nsa_select(imp, q_pos, n_slc):
    own = q_pos // SEL_BLOCK
    j = jnp.arange(n_slc, dtype=jnp.int32)[None, :]
    valid = j <= own[:, None]
    forced = (j == 0) | (j == own[:, None]) | (j == own[:, None] - 1)
    score = jnp.where(valid, imp + jnp.where(forced, SEL_BONUS, 0.0), -SEL_BONUS)
    n_top = min(N_SEL, n_slc)
    _, idx = lax.top_k(score, n_top)
    ok = idx <= own[None, None, :, None]
    onehot = (idx[..., None] == jnp.arange(n_slc, dtype=jnp.int32)) & ok[..., None]
    return jnp.any(onehot, axis=-2)


def _moba_select(sc, q_pos, n_blk):
    own = q_pos // MOBA_BLOCK
    past_blk = jnp.arange(n_blk, dtype=jnp.int32)[None, :] < own[:, None]
    sc = jnp.where(past_blk[None, None], sc, NEG_INF)
    n_top = min(MOBA_TOPK, n_blk)
    _, top = lax.top_k(sc, n_top)
    ok = top < own[None, None, :, None]
    onehot = (top[..., None] == jnp.arange(n_blk, dtype=jnp.int32)) & ok[..., None]
    return jnp.any(onehot, axis=-2)


def _pad_lanes(x, n=LANES):
    return jnp.pad(x, [(0, 0)] * (x.ndim - 1) + [(0, n - x.shape[-1])])


def _compress(k, pe, w1, w2):
    b, tk, g, d = k.shape
    n_cmp = (tk - CMP_LEN) // CMP_STRIDE + 1
    idx = np.arange(n_cmp)[:, None] * CMP_STRIDE + np.arange(CMP_LEN)[None, :]
    blk = k[:, idx] + pe[None, None, :, None, :]
    blk = jnp.transpose(blk, (0, 1, 3, 2, 4)).reshape(b * n_cmp * g, CMP_LEN * d)
    return cmp_mlp(blk, w1, w2).reshape(b, n_cmp, g, d)


def _gather_pages(pool, page_table):
    g = pool[page_table]
    return g.reshape((g.shape[0], g.shape[1] * g.shape[2], -1))


def _qbd(q, n_heads, d_head, scale):
    b, tq, c = q.shape
    qh = q.reshape(b, tq, n_heads, d_head) * scale
    eye = jnp.eye(n_heads, dtype=q.dtype)
    bd = jnp.einsum("bthd,hg->bhtgd", qh, eye).reshape(b, n_heads * tq, c)
    return jnp.pad(bd, ((0, 0), (0, LANES - n_heads * tq), (0, 0))).astype(BF)


def _diag_blocks(o, n_heads, tq, d_out):
    b = o.shape[0]
    o = o[:, :n_heads * tq].reshape(b, n_heads, tq, n_heads, d_out)
    idx = jnp.arange(n_heads)
    return jnp.transpose(o[:, idx, :, idx], (1, 2, 0, 3))


def _head_rows(x, tq):
    b, h, t, r = x.shape
    return jnp.pad(x.reshape(b, h * t, r), ((0, 0), (0, LANES - h * t), (0, 0)))


def kernel(x_prompt, x_sample, cache_diff_kv, cache_fox_kv, cache_fox_logf, cache_nsa_kv, cache_nsa_win,
           cache_moba_kv, cache_mem_kv, page_table, mem_prompt, norm_mix_g, w_in, b_forget, diff_lambda,
           diff_subln_g, nsa_cmp_pe, nsa_cmp_w1, nsa_cmp_w2, w_branch, w_out, norm_mem_g, norm_memkv_g,
           w_mem_q, w_mem_kv, w_mem_o, norm_ffn_g, w_router_group, b_router_group, w_router_expert,
           b_router_expert, w_exp_gate, w_exp_up, w_exp_down, norm_final_g):
    depth = w_in.shape[0]
    bp, t, d_model = x_prompt.shape
    bs, ts, _ = x_sample.shape
    page = cache_diff_kv.shape[2]
    past_len = page_table.shape[1] * page
    bw = d_model // N_BRANCH
    h_diff = bw // (2 * HEAD_DIM)
    h8 = bw // HEAD_DIM
    n_mem = mem_prompt.shape[1]
    mem_hd = bw // H_MEM
    n_exp = N_GROUPS * EXPERTS_PER_GROUP
    np_tok = bp * t
    ns_tok = bs * ts
    n_tok = np_tok + ns_tok
    assert h8 * ts <= LANES and 2 * h_diff * ts <= LANES and ts <= 8
    scale = HEAD_DIM ** -0.5

    pos_p = np.arange(t, dtype=np.int32)
    pos_s = past_len + np.arange(ts, dtype=np.int32)
    pos_all = np.concatenate([np.tile(pos_p, bp), np.tile(pos_s, bs)])
    cos, sin = _rope_tables(pos_all)

    sizes = (bw, bw, bw, bw, bw, bw, h8, bw, 6 * G_NSA * HEAD_DIM, 3 * h8, bw, bw, bw, N_BRANCH * d_model)
    offs = np.concatenate([[0], np.cumsum(sizes)])
    seg = lambda i: (int(offs[i]), int(offs[i + 1]))
    main_ids = (0, 1, 2, 3, 4, 5, 7, 8, 10, 11, 12)
    small_ids = (6, 9)
    n_main = sum(sizes[i] for i in main_ids)
    n_small = sum(sizes[i] for i in small_ids)
    n_main_pad = -(-(n_main + n_small) // 512) * 512
    mo = {}
    o = 0
    for i in main_ids + small_ids:
        mo[i] = o
        o += sizes[i]

    x = jnp.concatenate([x_prompt.reshape(np_tok, d_model), x_sample.reshape(ns_tok, d_model)], axis=0)

    e_moba = jnp.asarray(_block_expand(LANES, MOBA_BLOCK, t), BF)
    e_nsa = jnp.asarray(_block_expand(LANES, SEL_BLOCK, t), BF)
    n_cmp_p = (t - CMP_LEN) // CMP_STRIDE + 1
    n_cmp_p_pad = -(-n_cmp_p // LANES) * LANES
    n_slc_p = -(-t // SEL_BLOCK)
    ov_p = np.zeros((n_cmp_p_pad, LANES), np.float32)
    ov_p[:n_cmp_p, :n_slc_p] = _overlap_matrix(n_cmp_p, n_slc_p)
    ov_p = jnp.asarray(ov_p, BF)
    tk_s = past_len + ts
    n_cmp_s = (tk_s - CMP_LEN) // CMP_STRIDE + 1
    n_cmp_s_pad = -(-n_cmp_s // LANES) * LANES
    n_slc_s = -(-tk_s // SEL_BLOCK)
    ov_s = jnp.asarray(_overlap_matrix(n_cmp_s, n_slc_s))
    n_mblk_p = -(-t // MOBA_BLOCK)
    n_mblk_s = -(-tk_s // MOBA_BLOCK)

    acc_p = [[] for _ in range(7)]
    acc_s = [[] for _ in range(6)]

    for l in range(depth):
        lam_init = 0.8 - 0.6 * math.exp(-0.3 * l)
        lam_p = diff_lambda[l].astype(F32)
        lam = jnp.exp(jnp.sum(lam_p[0] * lam_p[1])) - jnp.exp(jnp.sum(lam_p[2] * lam_p[3])) + lam_init

        wl = w_in[l]
        w_main = jnp.concatenate([wl[:, seg(i)[0]:seg(i)[1]] for i in main_ids + small_ids], axis=1)
        w_main = jnp.pad(w_main, ((0, 0), (0, n_main_pad - n_main - n_small))).astype(BF)
        w_gate = wl[:, seg(13)[0]:seg(13)[1]].astype(BF)
        h = rmsnorm(x, norm_mix_g[l], BF)
        proj = matmul(h, w_main)
        glog = matmul(h, w_gate, tn_target=1024)

        def col(i):
            return proj[:, mo[i]:mo[i] + sizes[i]]

        dq = _rope(col(0).reshape(n_tok, 2 * h_diff, HEAD_DIM), cos, sin).reshape(n_tok, bw)
        dk = _rope(col(1).reshape(n_tok, 2 * h_diff, HEAD_DIM), cos, sin).reshape(n_tok, bw)
        diff_new = jnp.concatenate([dk, col(2)], axis=1)
        fq = col(3)
        fox_new = jnp.concatenate([col(4), col(5)], axis=1)
        logf_new = jax.nn.log_sigmoid(col(6).astype(F32) + b_forget[l].astype(F32))
        nq = _rope(col(7).reshape(n_tok, h8, HEAD_DIM), cos, sin).reshape(n_tok, bw)
        nkv = col(8).reshape(n_tok, 6, G_NSA, HEAD_DIM)
        nsa_new = jnp.stack([nkv[:, 0], nkv[:, 1], _rope(nkv[:, 2], cos, sin), nkv[:, 3]], axis=1)
        win_new = jnp.stack([_rope(nkv[:, 4], cos, sin), nkv[:, 5]], axis=1)
        ngate = jax.nn.sigmoid(col(9).astype(F32)).reshape(n_tok, h8, 3)
        mq = _rope(col(10).reshape(n_tok, h8, HEAD_DIM), cos, sin).reshape(n_tok, bw)
        mk = _rope(col(11).reshape(n_tok, h8, HEAD_DIM), cos, sin).reshape(n_tok, bw)
        moba_new = jnp.concatenate([mk, col(12)], axis=1)
        nsa_flat = nsa_new.reshape(n_tok, 4 * G_NSA * HEAD_DIM)
        win_flat = win_new.reshape(n_tok, 2 * G_NSA * HEAD_DIM)

        P = slice(0, np_tok)
        o_a_p = flash_prompt(dq[P], diff_new[P], diff_new[P], n_batch=bp, t_q=t, t_k=t, n_cols=h_diff,
                             q_col=lambda c: c, k_col=lambda c: c, v_col=lambda c: h_diff + c, mode="diff",
                             scale=scale, lam=lam, subln_g=diff_subln_g[l], lam_init=lam_init)
        logf_p = logf_new[P].reshape(bp, t, h8)
        c_all = jnp.cumsum(logf_p, axis=1)
        cq = jnp.transpose(c_all.reshape(bp, t, h8 // 2, 2), (0, 2, 1, 3))
        ck = jnp.transpose(c_all.reshape(bp, t, h8 // 2, 2), (0, 2, 3, 1))
        o_b_p = flash_prompt(fq[P], fox_new[P], fox_new[P], n_batch=bp, t_q=t, t_k=t, n_cols=h8 // 2,
                             q_col=lambda c: c, k_col=lambda c: c, v_col=lambda c: h8 // 2 + c, mode="pair",
                             scale=scale, bias=(cq, ck))
        nsa_p = nsa_new[P].reshape(bp, t, 4, G_NSA, HEAD_DIM)
        k_cmp = _compress(nsa_p[:, :, 0], nsa_cmp_pe[l, 0], nsa_cmp_w1[l, 0], nsa_cmp_w2[l, 0])
        v_cmp = _compress(nsa_p[:, :, 1], nsa_cmp_pe[l, 1], nsa_cmp_w1[l, 1], nsa_cmp_w2[l, 1])

        def dup_cmp(a):
            a = jnp.transpose(a, (0, 2, 1, 3))
            a = jnp.concatenate([a, a], axis=-1)
            return jnp.pad(a, ((0, 0), (0, 0), (0, n_cmp_p_pad - n_cmp_p), (0, 0)))

        o_cmp_p, imp_p = cmp_attn_prompt(nq[P], dup_cmp(k_cmp), dup_cmp(v_cmp), ov_p, n_batch=bp, t_q=t)
        sel_nsa = _nsa_select(imp_p[..., :n_slc_p], jnp.asarray(pos_p), n_slc_p)
        sel_nsa = _pad_lanes(sel_nsa.astype(BF))

        def dup_g(a):
            return jnp.concatenate([a, a], axis=-1).reshape(a.shape[0], G_NSA * LANES)

        ks_dup, vs_dup = dup_g(nsa_p.reshape(np_tok, 4, G_NSA, HEAD_DIM)[:, 2]), dup_g(nsa_p.reshape(np_tok, 4, G_NSA, HEAD_DIM)[:, 3])
        ppg = h8 // G_NSA // 2
        o_sel_p = flash_prompt(nq[P], ks_dup, vs_dup, n_batch=bp, t_q=t, t_k=t, n_cols=h8 // 2,
                               q_col=lambda c: c, k_col=lambda c: c // ppg, v_col=lambda c: c // ppg, mode="pair",
                               scale=scale, sel=sel_nsa, sel_mode="per_group", emat=e_nsa)
        win_p = win_new[P]
        kw_dup, vw_dup = dup_g(win_p[:, 0]), dup_g(win_p[:, 1])
        o_win_p = flash_prompt(nq[P], kw_dup, vw_dup, n_batch=bp, t_q=t, t_k=t, n_cols=h8 // 2,
                               q_col=lambda c: c, k_col=lambda c: c // ppg, v_col=lambda c: c // ppg, mode="pair",
                               scale=scale, window=True)
        mk_p = mk[P].reshape(bp, n_mblk_p, MOBA_BLOCK, h8, HEAD_DIM)
        k_mean_p = jnp.transpose(jnp.mean(mk_p.astype(F32), axis=2), (0, 2, 1, 3))
        sc_p = jnp.einsum("bqhd,bhnd->bhqn", mq[P].reshape(bp, t, h8, HEAD_DIM).astype(F32), k_mean_p)
        sel_moba = _moba_select(sc_p, jnp.asarray(pos_p), n_mblk_p)
        own_p = jnp.asarray(pos_p // MOBA_BLOCK)
        sel_moba = sel_moba | (jnp.arange(n_mblk_p)[None, :] == own_p[:, None])[None, None]
        sel_moba = _pad_lanes(sel_moba.astype(BF))
        o_d_p = flash_prompt(mq[P], moba_new[P], moba_new[P], n_batch=bp, t_q=t, t_k=t, n_cols=h8 // 2,
                             q_col=lambda c: c, k_col=lambda c: c, v_col=lambda c: h8 // 2 + c, mode="pair",
                             scale=scale, sel=sel_moba, sel_mode="per_head", emat=e_moba)

        S = slice(np_tok, n_tok)

        def new_rows(a):
            a = a.reshape(bs, ts, a.shape[-1])
            return jnp.pad(a, ((0, 0), (0, NEW_PAD - ts), (0, 0)))

        def new_bias(n_heads):
            row_t = np.arange(LANES) % ts
            keyi = np.arange(NEW_PAD)
            okm = (keyi[None, :] <= row_t[:, None]) & (keyi[None, :] < ts)
            okm = okm & (np.arange(LANES)[:, None] < n_heads * ts)
            okm = okm | ((np.arange(LANES)[:, None] >= n_heads * ts) & (keyi[None, :] == 0))
            return jnp.broadcast_to(jnp.asarray(np.where(okm, 0.0, NEG_INF), F32), (bs, LANES, NEW_PAD))

        past_diff = _gather_pages(cache_diff_kv[l], page_table)
        qbd = _qbd(dq[S].reshape(bs, ts, bw), 2 * h_diff, HEAD_DIM, scale)
        o = decode_attn(qbd, past_diff, past_diff, k_col=0, v_col=1, ck=bw, cv=bw,
                        new=(new_rows(diff_new[S]), new_rows(diff_new[S]), new_bias(2 * h_diff)))
        o = o[:, :2 * h_diff * ts].reshape(bs, h_diff, 2, ts, h_diff, 2 * HEAD_DIM)
        hi_ = jnp.arange(h_diff)
        o = o[:, hi_, :, :, hi_]
        o = jnp.transpose(o[:, :, 0] - lam * o[:, :, 1], (1, 2, 0, 3))
        y = o * lax.rsqrt(jnp.mean(o * o, axis=-1, keepdims=True) + NORM_EPS)
        o_a_s = ((y * diff_subln_g[l].astype(F32)) * (1.0 - lam_init)).reshape(ns_tok, bw)

        past_fox = _gather_pages(cache_fox_kv[l], page_table)
        past_logf = _gather_pages(cache_fox_logf[l], page_table)
        logf_s = logf_new[S].reshape(bs, ts, h8)
        c_s = jnp.cumsum(jnp.concatenate([past_logf.astype(F32), logf_s], axis=1), axis=1)
        c_q = c_s[:, past_len:]
        bias_all = jnp.transpose(c_q, (0, 2, 1))[..., None] - jnp.transpose(c_s, (0, 2, 1))[:, :, None, :]
        bias_rows = _head_rows(bias_all, ts)
        bias_new = _pad_lanes(bias_rows[..., past_len:], NEW_PAD) + new_bias(h8)
        qbd = _qbd(fq[S].reshape(bs, ts, bw), h8, HEAD_DIM, scale)
        o = decode_attn(qbd, past_fox, past_fox, k_col=0, v_col=1, ck=bw, cv=bw, bias=bias_rows[..., :past_len],
                        new=(new_rows(fox_new[S]), new_rows(fox_new[S]), bias_new))
        o_b_s = _diag_blocks(o, h8, ts, HEAD_DIM).reshape(ns_tok, bw)

        past_nsa = _gather_pages(cache_nsa_kv[l], page_table)
        gw = G_NSA * HEAD_DIM
        nsa_s = nsa_flat[S].reshape(bs, ts, 4 * gw)
        nsa_all = jnp.concatenate([past_nsa, nsa_s], axis=1).reshape(bs, tk_s, 4, G_NSA, HEAD_DIM)
        k_cmp_s = _compress(nsa_all[:, :, 0], nsa_cmp_pe[l, 0], nsa_cmp_w1[l, 0], nsa_cmp_w2[l, 0])
        v_cmp_s = _compress(nsa_all[:, :, 1], nsa_cmp_pe[l, 1], nsa_cmp_w1[l, 1], nsa_cmp_w2[l, 1])
        kv_cmp_s = jnp.concatenate([k_cmp_s.reshape(bs, n_cmp_s, gw), v_cmp_s.reshape(bs, n_cmp_s, gw)], axis=-1)
        kv_cmp_s = jnp.pad(kv_cmp_s, ((0, 0), (0, n_cmp_s_pad - n_cmp_s), (0, 0)))
        ends = np.arange(n_cmp_s_pad) * CMP_STRIDE + (CMP_LEN - 1)
        row_pos = past_len + (np.arange(LANES) % ts)
        cmp_ok = (ends[None, :] <= row_pos[:, None]) & (np.arange(n_cmp_s_pad)[None, :] < n_cmp_s)
        cmp_bias = jnp.broadcast_to(jnp.asarray(np.where(cmp_ok, 0.0, NEG_INF), F32), (bs, LANES, n_cmp_s_pad))
        qbd_n = _qbd(nq[S].reshape(bs, ts, bw), h8, HEAD_DIM, scale)
        qbd_g = qbd_n.reshape(bs, LANES, G_NSA, h8 // G_NSA, HEAD_DIM).sum(axis=3).reshape(bs, LANES, gw)
        o, p_c = decode_attn(qbd_g, kv_cmp_s, kv_cmp_s, k_col=0, v_col=1, ck=gw, cv=gw, bias=cmp_bias,
                             emit_p=True, rk=n_cmp_s_pad)
        hpg = h8 // G_NSA
        o_cmp_s = o[:, :h8 * ts].reshape(bs, G_NSA, hpg, ts, G_NSA, HEAD_DIM)
        gi_ = jnp.arange(G_NSA)
        o_cmp_s = jnp.transpose(o_cmp_s[:, gi_, :, :, gi_], (1, 3, 0, 2, 4)).reshape(ns_tok, bw)
        p_c = p_c[:, :h8 * ts, :n_cmp_s].reshape(bs, G_NSA, hpg, ts, n_cmp_s)
        imp_s = jnp.einsum("bgmqn,ns->bgqs", p_c, ov_s)
        sel_s = _nsa_select(imp_s, jnp.asarray(pos_s), n_slc_s)
        key_blk = np.arange(past_len) // SEL_BLOCK
        sel_rows = jnp.broadcast_to(sel_s[:, :, None], (bs, G_NSA, hpg, ts, n_slc_s)).reshape(bs, h8, ts, n_slc_s)
        sel_bias = jnp.where(sel_rows[..., key_blk], 0.0, NEG_INF).astype(F32)
        sel_bias = _head_rows(sel_bias, ts)
        o = decode_attn(qbd_g, past_nsa, past_nsa, k_col=2, v_col=3, ck=gw, cv=gw, bias=sel_bias,
                        new=(new_rows(nsa_flat[S]), new_rows(nsa_flat[S]), new_bias(h8)))
        o_sel_s = o[:, :h8 * ts].reshape(bs, G_NSA, hpg, ts, G_NSA, HEAD_DIM)
        o_sel_s = jnp.transpose(o_sel_s[:, gi_, :, :, gi_], (1, 3, 0, 2, 4)).reshape(ns_tok, bw)
        wbuf = cache_nsa_win[l].reshape(bs, -1, 2 * gw)
        wb = wbuf.shape[1]
        win_ok = (np.arange(wb)[None, :] >= (np.arange(LANES) % ts)[:, None] + (wb - WINDOW)) | \
                 (np.arange(LANES)[:, None] >= h8 * ts)
        win_bias = jnp.broadcast_to(jnp.asarray(np.where(win_ok, 0.0, NEG_INF), F32), (bs, LANES, wb))
        o = decode_attn(qbd_g, wbuf, wbuf, k_col=0, v_col=1, ck=gw, cv=gw, bias=win_bias,
                        new=(new_rows(win_flat[S]), new_rows(win_flat[S]), new_bias(h8)), rk=wb)
        o_win_s = o[:, :h8 * ts].reshape(bs, G_NSA, hpg, ts, G_NSA, HEAD_DIM)
        o_win_s = jnp.transpose(o_win_s[:, gi_, :, :, gi_], (1, 3, 0, 2, 4)).reshape(ns_tok, bw)
        win_state = jnp.concatenate([wbuf, win_flat[S].reshape(bs, ts, 2 * gw)], axis=1)[:, ts:]

        past_moba = _gather_pages(cache_moba_kv[l], page_table)
        n_past_blk = past_len // MOBA_BLOCK
        k_mean_s = jnp.mean(past_moba[..., :bw].reshape(bs, n_past_blk, MOBA_BLOCK, h8, HEAD_DIM).astype(F32), axis=2)
        k_mean_s = jnp.transpose(k_mean_s, (0, 2, 1, 3))
        k_mean_s = jnp.pad(k_mean_s, ((0, 0), (0, 0), (0, n_mblk_s - n_past_blk), (0, 0)))
        sc_s = jnp.einsum("bqhd,bhnd->bhqn", mq[S].reshape(bs, ts, h8, HEAD_DIM).astype(F32), k_mean_s)
        selm_s = _moba_select(sc_s, jnp.asarray(pos_s), n_mblk_s)
        mkey_blk = np.arange(past_len) // MOBA_BLOCK
        moba_bias = _head_rows(jnp.where(selm_s[..., mkey_blk], 0.0, NEG_INF).astype(F32), ts)
        qbd = _qbd(mq[S].reshape(bs, ts, bw), h8, HEAD_DIM, scale)
        o = decode_attn(qbd, past_moba, past_moba, k_col=0, v_col=1, ck=bw, cv=bw, bias=moba_bias,
                        new=(new_rows(moba_new[S]), new_rows(moba_new[S]), new_bias(h8)))
        o_d_s = _diag_blocks(o, h8, ts, HEAD_DIM).reshape(ns_tok, bw)

        o_a = jnp.concatenate([o_a_p, o_a_s], axis=0)
        o_b = jnp.concatenate([o_b_p, o_b_s], axis=0)
        o_cmp = jnp.concatenate([o_cmp_p, o_cmp_s], axis=0).reshape(n_tok, h8, HEAD_DIM)
        o_sel = jnp.concatenate([o_sel_p, o_sel_s], axis=0).reshape(n_tok, h8, HEAD_DIM)
        o_win = jnp.concatenate([o_win_p, o_win_s], axis=0).reshape(n_tok, h8, HEAD_DIM)
        o_c = (ngate[..., 0:1] * o_cmp + ngate[..., 1:2] * o_sel + ngate[..., 2:3] * o_win).reshape(n_tok, bw)
        o_d = jnp.concatenate([o_d_p, o_d_s], axis=0)
        merged = merge_branches([o_a, o_b, o_c, o_d], w_branch[l], glog, d_model)
        x = matmul(merged, w_out[l], res=x)

        mem_h = rmsnorm(mem_prompt.reshape(bp * n_mem, d_model), norm_memkv_g[l], BF)
        mem_kv_p = matmul(mem_h, w_mem_kv[l])
        hm = rmsnorm(x, norm_mem_g[l], BF)
        qm = matmul(hm, w_mem_q[l])
        o_m_p = flash_prompt(qm[P], mem_kv_p, mem_kv_p, n_batch=bp, t_q=t, t_k=n_mem, n_cols=H_MEM,
                             q_col=lambda c: c, k_col=lambda c: c, v_col=lambda c: H_MEM + c, mode="full",
                             scale=mem_hd ** -0.5, causal=False)
        qbd = _qbd(qm[S].reshape(bs, ts, bw), H_MEM, mem_hd, mem_hd ** -0.5)
        mem_s = cache_mem_kv[l].reshape(bs, n_mem, 2 * bw)
        o = decode_attn(qbd, mem_s, mem_s, k_col=0, v_col=1, ck=bw, cv=bw, rk=n_mem)
        o_m_s = _diag_blocks(o, H_MEM, ts, mem_hd).reshape(ns_tok, bw)
        x = matmul(jnp.concatenate([o_m_p, o_m_s], axis=0), w_mem_o[l], res=x)

        hf = rmsnorm(x, norm_ffn_g[l], BF)
        w_r = _pad_lanes(jnp.concatenate([w_router_group[l], w_router_expert[l]], axis=1))
        logits = matmul(hf, w_r, tn_target=LANES)
        rows = jnp.arange(n_tok)
        lg = logits[:, :N_GROUPS] + b_router_group[l].astype(F32)
        grp = jnp.argmax(lg, axis=-1).astype(jnp.int32)
        p_grp = jax.nn.softmax(lg, axis=-1)[rows, grp]
        le = (logits[:, N_GROUPS:N_GROUPS + n_exp] + b_router_expert[l].astype(F32)).reshape(n_tok, N_GROUPS, EXPERTS_PER_GROUP)
        p_exp = jax.nn.softmax(le[rows, grp], axis=-1)
        w_top, i_top = lax.top_k(p_exp, TOPK_IN_GROUP)
        gate = p_grp[:, None] * w_top / jnp.sum(w_top, axis=-1, keepdims=True)
        eid = grp[:, None] * EXPERTS_PER_GROUP + i_top.astype(jnp.int32)
        kk = TOPK_IN_GROUP
        m_asg = n_tok * kk
        flat_e = eid.reshape(m_asg)
        order = jnp.argsort(flat_e)
        se = flat_e[order]
        counts = jnp.bincount(flat_e, length=n_exp).astype(jnp.int32)
        padded = ((counts + MOE_BLOCK - 1) // MOE_BLOCK) * MOE_BLOCK
        pad_end = jnp.cumsum(padded)
        pad_start = pad_end - padded
        start = jnp.cumsum(counts) - counts
        dest = pad_start[se] + (jnp.arange(m_asg, dtype=jnp.int32) - start[se])
        n_blk = -(-(m_asg + n_exp * (MOE_BLOCK - 1)) // MOE_BLOCK)
        tok = order // kk
        xs = jnp.zeros((n_blk * MOE_BLOCK, d_model), BF).at[dest].set(hf[tok])
        blk_e = jnp.minimum(jnp.searchsorted(pad_end, jnp.arange(n_blk, dtype=jnp.int32) * MOE_BLOCK, side="right"),
                            n_exp - 1).astype(jnp.int32)
        ys = moe_experts(xs, blk_e, w_exp_gate[l], w_exp_up[l], w_exp_down[l])
        dest_flat = jnp.zeros((m_asg,), jnp.int32).at[order].set(dest)
        contrib = ys[dest_flat].reshape(n_tok, kk, d_model) * gate[..., None].astype(F32)
        x = x + jnp.sum(contrib, axis=1)

        def pr(a, shape):
            return a[P].reshape((bp, t) + shape)

        def sr(a, shape):
            return a[S].reshape((bs, ts) + shape)

        acc_p[0].append(pr(diff_new, (2, h_diff, 2 * HEAD_DIM)))
        acc_p[1].append(pr(fox_new, (2, h8, HEAD_DIM)))
        acc_p[2].append(pr(logf_new, (h8,)))
        acc_p[3].append(pr(nsa_flat, (4, G_NSA, HEAD_DIM)))
        acc_p[4].append(pr(win_flat, (2, G_NSA, HEAD_DIM))[:, t - min(WINDOW, t):])
        acc_p[5].append(pr(moba_new, (2, h8, HEAD_DIM)))
        acc_p[6].append(mem_kv_p.reshape(bp, n_mem, 2, H_MEM, mem_hd))
        acc_s[0].append(sr(diff_new, (2, h_diff, 2 * HEAD_DIM)))
        acc_s[1].append(sr(fox_new, (2, h8, HEAD_DIM)))
        acc_s[2].append(sr(logf_new, (h8,)))
        acc_s[3].append(sr(nsa_flat, (4, G_NSA, HEAD_DIM)))
        acc_s[4].append(win_state.reshape(bs, wb, 2, G_NSA, HEAD_DIM))
        acc_s[5].append(sr(moba_new, (2, h8, HEAD_DIM)))

    y = rmsnorm(x, norm_final_g, F32)
    y_prompt = y[:np_tok].reshape(bp, t, d_model)
    y_sample = y[np_tok:].reshape(bs, ts, d_model)
    return (y_prompt, y_sample) + tuple(jnp.stack(a) for a in acc_p) + tuple(jnp.stack(a) for a in acc_s)
```

```python
import functools
import math

import numpy as np
import jax
import jax.numpy as jnp
from jax import lax
from jax.experimental import pallas as pl
from jax.experimental.pallas import tpu as pltpu

F32 = jnp.float32
BF = jnp.bfloat16

HEAD_DIM = 64
ROPE_DIMS = HEAD_DIM // 4
ROPE_THETA = 500000.0
N_BRANCH = 4
NORM_EPS = 1e-6
NEG_INF = -1e30
G_NSA = 2
CMP_LEN = 32
CMP_STRIDE = 16
SEL_BLOCK = 64
N_SEL = 16
WINDOW = 512
SEL_BONUS = 1e4
MOBA_BLOCK = 256
MOBA_TOPK = 3
H_MEM = 4
N_GROUPS = 4
EXPERTS_PER_GROUP = 8
TOPK_IN_GROUP = 2
MOE_BLOCK = 128
LANES = 128
NEW_PAD = 128

VMEM_LIMIT = 56 * 1024 * 1024


def _cparams(*sem):
    return pltpu.CompilerParams(dimension_semantics=sem, vmem_limit_bytes=VMEM_LIMIT)


def _tile(n, target, mult=8):
    best = None
    for t in range(mult, min(n, target) + 1, mult):
        if n % t == 0:
            best = t
    return best if best is not None else n


def _rms_kernel(x_ref, g_ref, o_ref):
    x = x_ref[...].astype(F32)
    r = lax.rsqrt(jnp.mean(x * x, axis=-1, keepdims=True) + NORM_EPS)
    o_ref[...] = ((x * r) * g_ref[...]).astype(o_ref.dtype)


def rmsnorm(x, g, dtype):
    n, d = x.shape
    tm = _tile(n, 512)
    return pl.pallas_call(
        _rms_kernel,
        grid=(n // tm,),
        in_specs=[pl.BlockSpec((tm, d), lambda i: (i, 0)), pl.BlockSpec((1, d), lambda i: (0, 0))],
        out_specs=pl.BlockSpec((tm, d), lambda i: (i, 0)),
        out_shape=jax.ShapeDtypeStruct((n, d), dtype),
        compiler_params=_cparams("parallel"),
        name="rmsnorm",
    )(x, g.reshape(1, d).astype(F32))


def _mm_kernel(a_ref, w_ref, o_ref):
    o_ref[...] = jnp.dot(a_ref[...].astype(BF), w_ref[...].astype(BF),
                         preferred_element_type=F32).astype(o_ref.dtype)


def _mm_res_kernel(a_ref, w_ref, r_ref, o_ref):
    y = jnp.dot(a_ref[...].astype(BF), w_ref[...].astype(BF), preferred_element_type=F32)
    o_ref[...] = (r_ref[...] + y).astype(o_ref.dtype)


def matmul(a, w, res=None, out_dtype=F32, tm_target=640, tn_target=512):
    m, k = a.shape
    n = w.shape[1]
    tm = _tile(m, tm_target)
    tn = _tile(n, tn_target, LANES)
    in_specs = [pl.BlockSpec((tm, k), lambda i, j: (i, 0)), pl.BlockSpec((k, tn), lambda i, j: (0, j))]
    args = [a, w]
    kern = _mm_kernel
    if res is not None:
        in_specs.append(pl.BlockSpec((tm, tn), lambda i, j: (i, j)))
        args.append(res)
        kern = _mm_res_kernel
    return pl.pallas_call(
        kern,
        grid=(m // tm, n // tn),
        in_specs=in_specs,
        out_specs=pl.BlockSpec((tm, tn), lambda i, j: (i, j)),
        out_shape=jax.ShapeDtypeStruct((m, n), out_dtype),
        compiler_params=_cparams("parallel", "arbitrary"),
        name="matmul",
    )(*args)


def _cmp_mlp_kernel(x_ref, w1_ref, w2_ref, o_ref):
    h = jnp.dot(x_ref[...].astype(BF), w1_ref[...].astype(BF), preferred_element_type=F32)
    h = jax.nn.gelu(h)
    o_ref[...] = jnp.dot(h.astype(BF), w2_ref[...].astype(BF), preferred_element_type=F32)


def cmp_mlp(x, w1, w2):
    r, k = x.shape
    f = w1.shape[1]
    d = w2.shape[1]
    tm = _tile(r, 512)
    return pl.pallas_call(
        _cmp_mlp_kernel,
        grid=(r // tm,),
        in_specs=[pl.BlockSpec((tm, k), lambda i: (i, 0)), pl.BlockSpec((k, f), lambda i: (0, 0)),
                  pl.BlockSpec((f, d), lambda i: (0, 0))],
        out_specs=pl.BlockSpec((tm, d), lambda i: (i, 0)),
        out_shape=jax.ShapeDtypeStruct((r, d), F32),
        compiler_params=_cparams("parallel"),
        name="cmp_mlp",
    )(x, w1, w2)


def _cmp_finish_kernel(x_ref, pe_ref, w1_ref, w2_ref, o_ref):
    pe_term = jnp.dot(pe_ref[...].astype(BF), w1_ref[...].astype(BF), preferred_element_type=F32)[0:1]
    h = jax.nn.gelu(x_ref[...] + pe_term)
    o_ref[...] = jnp.dot(h.astype(BF), w2_ref[...].astype(BF), preferred_element_type=F32)


def cmp_finish(x, pe_flat, w1, w2):
    r, f = x.shape
    k = w1.shape[0]
    d = w2.shape[1]
    tm = _tile(r, 1024)
    pe8 = jnp.broadcast_to(pe_flat.reshape(1, k), (8, k))
    return pl.pallas_call(
        _cmp_finish_kernel,
        grid=(r // tm,),
        in_specs=[pl.BlockSpec((tm, f), lambda i: (i, 0)), pl.BlockSpec((8, k), lambda i: (0, 0)),
                  pl.BlockSpec((k, f), lambda i: (0, 0)), pl.BlockSpec((f, d), lambda i: (0, 0))],
        out_specs=pl.BlockSpec((tm, d), lambda i: (i, 0)),
        out_shape=jax.ShapeDtypeStruct((r, d), F32),
        compiler_params=_cparams("parallel"),
        name="cmp_finish",
    )(x, pe8, w1, w2)


def _flash_kernel(*refs, mode, tq, tk, n_k, scale, causal, window, bias, sel, lam_init):
    it = iter(refs)
    q_ref, k_ref, vt_ref = next(it), next(it), next(it)
    cq_ref = ck_ref = sel_ref = e_ref = lam_ref = g_ref = None
    if bias:
        cq_ref, ck_ref = next(it), next(it)
    if sel:
        sel_ref, e_ref = next(it), next(it)
    if mode == "diff":
        lam_ref, g_ref = next(it), next(it)
    o_ref, m_ref, l_ref, acc_ref = next(it), next(it), next(it), next(it)

    n_sub = 1 if mode == "full" else 2
    qi = pl.program_id(2)
    lane = lax.broadcasted_iota(jnp.int32, (1, LANES), 1)
    q = q_ref[...].astype(F32) * scale
    if n_sub == 2:
        qs = [jnp.where(lane < HEAD_DIM, q, 0.0).astype(BF), jnp.where(lane >= HEAD_DIM, q, 0.0).astype(BF)]
    else:
        qs = [q.astype(BF)]
    qpos = qi * tq + lax.broadcasted_iota(jnp.int32, (1, tq), 1)

    m_ref[...] = jnp.full(m_ref.shape, NEG_INF, F32)
    l_ref[...] = jnp.zeros(l_ref.shape, F32)
    acc_ref[...] = jnp.zeros(acc_ref.shape, F32)

    if sel:
        sels = [sel_ref[e if sel == "per_head" else 0] for e in range(n_sub)]

    def chunk(j, masked):
        off = pl.multiple_of(j * tk, tk)
        kb = k_ref[pl.ds(off, tk), :]
        vt = vt_ref[:, pl.ds(off, tk)]
        mask = None
        if masked:
            kpos = off + lax.broadcasted_iota(jnp.int32, (tk, 1), 0)
            mask = kpos <= qpos
            if window:
                mask = jnp.logical_and(mask, qpos - kpos <= WINDOW)
        ss = [lax.dot_general(kb, qs[e], (((1,), (1,)), ((), ())), preferred_element_type=F32)
              for e in range(n_sub)]
        mes = []
        for e in range(n_sub):
            if bias:
                ss[e] = ss[e] + (cq_ref[e:e + 1, :] - ck_ref[pl.ds(off, tk), e:e + 1])
            me = mask
            if sel:
                selm = jnp.dot(e_ref[pl.ds(off, tk), :], sels[e], preferred_element_type=F32) > 0.5
                me = selm if mask is None else jnp.logical_and(selm, mask)
            if me is not None:
                ss[e] = jnp.where(me, ss[e], NEG_INF)
            mes.append(me)
        m_prev = [m_ref[e] for e in range(n_sub)]
        m_new = [jnp.maximum(m_prev[e], jnp.max(ss[e], axis=0, keepdims=True)) for e in range(n_sub)]
        ps = []
        for e in range(n_sub):
            p = jnp.exp(ss[e] - m_new[e])
            if mes[e] is not None:
                p = jnp.where(mes[e], p, 0.0)
            ps.append(p)
        pvs = [jnp.dot(vt, ps[e].astype(BF), preferred_element_type=F32) for e in range(n_sub)]
        for e in range(n_sub):
            alpha = jnp.exp(m_prev[e] - m_new[e])
            l_ref[e] = alpha * l_ref[e] + jnp.sum(ps[e], axis=0, keepdims=True)
            acc_ref[e] = alpha * acc_ref[e] + pvs[e]
            m_ref[e] = m_new[e]

    def masked_body(j, carry):
        chunk(j, True)
        return carry

    def open_body(j, carry):
        chunk(j, False)
        return carry

    if not causal:
        lax.fori_loop(0, n_k, open_body, 0)
    elif window:
        lax.fori_loop(jnp.maximum(qi * tq - WINDOW, 0) // tk, ((qi + 1) * tq + tk - 1) // tk, masked_body, 0)
    else:
        n_open = (qi * tq) // tk
        lax.fori_loop(0, n_open, open_body, 0)
        lax.fori_loop(n_open, ((qi + 1) * tq + tk - 1) // tk, masked_body, 0)

    outs = []
    for e in range(n_sub):
        l = l_ref[e]
        outs.append(acc_ref[e] * jnp.where(l > 0.0, 1.0 / l, 0.0))
    if mode == "full":
        ot = outs[0]
    elif mode == "pair":
        sub = lax.broadcasted_iota(jnp.int32, (LANES, 1), 0)
        ot = jnp.where(sub < HEAD_DIM, outs[0], outs[1])
    else:
        ot = outs[0] - lam_ref[0] * outs[1]
    o = ot.T
    if mode == "diff":
        y = o * lax.rsqrt(jnp.mean(o * o, axis=-1, keepdims=True) + NORM_EPS)
        o = (y * g_ref[...]) * (1.0 - lam_init)
    o_ref[...] = o.astype(o_ref.dtype)


def flash_prompt(q, k, vt, *, n_batch, t_q, t_k, n_cols, q_col, k_col, v_col, mode, scale,
                 causal=True, window=False, bias=None, sel=None, sel_mode=None, emat=None,
                 lam=None, subln_g=None, lam_init=0.0, tq=256, tk=512):
    tq = min(tq, t_q)
    tk = min(tk, t_k)
    nq = t_q // tq
    n_k = t_k // tk
    in_specs = [
        pl.BlockSpec((tq, LANES), lambda b, c, i: (b * nq + i, q_col(c))),
        pl.BlockSpec((t_k, LANES), lambda b, c, i: (b, k_col(c))),
        pl.BlockSpec((None, LANES, t_k), lambda b, c, i: (b, v_col(c), 0)),
    ]
    args = [q, k, vt]
    if bias is not None:
        cq, ck = bias
        in_specs += [pl.BlockSpec((None, None, 2, tq), lambda b, c, i: (b, c, 0, i)),
                     pl.BlockSpec((None, None, t_k, 2), lambda b, c, i: (b, c, 0, 0))]
        args += [cq, ck]
    if sel is not None:
        if sel_mode == "per_head":
            in_specs.append(pl.BlockSpec((None, 2, LANES, tq), lambda b, c, i: (b, c, 0, i)))
        else:
            in_specs.append(pl.BlockSpec((None, 1, LANES, tq), lambda b, c, i: (b, c // 2, 0, i)))
        in_specs.append(pl.BlockSpec((t_k, LANES), lambda b, c, i: (0, 0)))
        args += [sel, emat]
    if mode == "diff":
        in_specs += [pl.BlockSpec(memory_space=pltpu.SMEM), pl.BlockSpec((1, LANES), lambda b, c, i: (0, 0))]
        args += [lam.reshape(1).astype(F32), subln_g.reshape(1, LANES).astype(F32)]
    n_sub = 1 if mode == "full" else 2
    kern = functools.partial(_flash_kernel, mode=mode, tq=tq, tk=tk, n_k=n_k, scale=scale, causal=causal,
                             window=window, bias=bias is not None, sel=sel_mode if sel is not None else None,
                             lam_init=lam_init)
    return pl.pallas_call(
        kern,
        grid=(n_batch, n_cols, nq),
        in_specs=in_specs,
        out_specs=pl.BlockSpec((tq, LANES), lambda b, c, i: (b * nq + i, c)),
        out_shape=jax.ShapeDtypeStruct((n_batch * t_q, n_cols * LANES), F32),
        scratch_shapes=[pltpu.VMEM((n_sub, 1, tq), F32), pltpu.VMEM((n_sub, 1, tq), F32),
                        pltpu.VMEM((n_sub, LANES, tq), F32)],
        compiler_params=_cparams("parallel", "parallel", "arbitrary"),
        name="flash_" + mode,
    )(*args)


def _cmp_attn_kernel(q_ref, k_ref, v_ref, ov_ref, o_ref, imp_ref, *, tq, scale):
    qi = pl.program_id(2)
    lane = lax.broadcasted_iota(jnp.int32, (1, LANES), 1)
    n_c = k_ref.shape[0]
    qpos = qi * tq + lax.broadcasted_iota(jnp.int32, (tq, 1), 0)
    ends = lax.broadcasted_iota(jnp.int32, (1, n_c), 1) * CMP_STRIDE + (CMP_LEN - 1)
    mask = ends <= qpos
    kb = k_ref[...].astype(BF)
    vb = v_ref[...].astype(BF)
    ov = ov_ref[...]
    imp = jnp.zeros((tq, LANES), F32)
    halves = []
    for pair in range(2):
        q = q_ref[:, pair * LANES:(pair + 1) * LANES].astype(F32) * scale
        outs = []
        for e in range(2):
            qe = (jnp.where(lane < HEAD_DIM, q, 0.0) if e == 0 else jnp.where(lane >= HEAD_DIM, q, 0.0)).astype(BF)
            s = lax.dot_general(qe, kb, (((1,), (1,)), ((), ())), preferred_element_type=F32)
            s = jnp.where(mask, s, NEG_INF)
            m = jnp.max(s, axis=-1, keepdims=True)
            p = jnp.where(mask, jnp.exp(s - m), 0.0)
            l = jnp.sum(p, axis=-1, keepdims=True)
            p = p * jnp.where(l > 0.0, 1.0 / l, 0.0)
            pb = p.astype(BF)
            outs.append(jnp.dot(pb, vb, preferred_element_type=F32))
            imp = imp + jnp.dot(pb, ov, preferred_element_type=F32)
        halves.append(jnp.where(lane < HEAD_DIM, outs[0], outs[1]))
    o_ref[:, 0:LANES] = halves[0]
    o_ref[:, LANES:2 * LANES] = halves[1]
    imp_ref[...] = imp


def cmp_attn_prompt(q, k_dup, v_dup, ov, *, n_batch, t_q, tq=256):
    tq = min(tq, t_q)
    nq = t_q // tq
    n_c = k_dup.shape[2]
    kern = functools.partial(_cmp_attn_kernel, tq=tq, scale=HEAD_DIM ** -0.5)
    return pl.pallas_call(
        kern,
        grid=(n_batch, G_NSA, nq),
        in_specs=[pl.BlockSpec((tq, 2 * LANES), lambda b, g, i: (b * nq + i, g)),
                  pl.BlockSpec((None, None, n_c, LANES), lambda b, g, i: (b, g, 0, 0)),
                  pl.BlockSpec((None, None, n_c, LANES), lambda b, g, i: (b, g, 0, 0)),
                  pl.BlockSpec((n_c, LANES), lambda b, g, i: (0, 0))],
        out_specs=[pl.BlockSpec((tq, 2 * LANES), lambda b, g, i: (b * nq + i, g)),
                   pl.BlockSpec((None, None, tq, LANES), lambda b, g, i: (b, g, i, 0))],
        out_shape=[jax.ShapeDtypeStruct((n_batch * t_q, G_NSA * 2 * LANES), F32),
                   jax.ShapeDtypeStruct((n_batch, G_NSA, t_q, LANES), F32)],
        compiler_params=_cparams("parallel", "parallel", "arbitrary"),
        name="cmp_attn",
    )(q, k_dup, v_dup, ov)


def _decode_kernel(*refs, has_bias, has_new, emit_p):
    it = iter(refs)
    q_ref, k_ref, v_ref = next(it), next(it), next(it)
    b_ref = next(it) if has_bias else None
    if has_new:
        kn_ref, vn_ref, bn_ref = next(it), next(it), next(it)
    o_ref = next(it)
    p_ref = next(it) if emit_p else None
    m_ref, l_ref, acc_ref = next(it), next(it), next(it)
    j = pl.program_id(1)
    last = pl.num_programs(1) - 1

    @pl.when(j == 0)
    def _():
        m_ref[...] = jnp.full(m_ref.shape, NEG_INF, F32)
        l_ref[...] = jnp.zeros(l_ref.shape, F32)
        acc_ref[...] = jnp.zeros(acc_ref.shape, F32)

    q = q_ref[...]

    def update(kb, vb, b):
        s = lax.dot_general(q, kb.astype(BF), (((1,), (1,)), ((), ())), preferred_element_type=F32)
        if b is not None:
            s = s + b
        m_prev = m_ref[...]
        m_new = jnp.maximum(m_prev, jnp.max(s, axis=-1, keepdims=True))
        alpha = jnp.exp(m_prev - m_new)
        p = jnp.exp(s - m_new)
        l_ref[...] = alpha * l_ref[...] + jnp.sum(p, axis=-1, keepdims=True)
        acc_ref[...] = alpha * acc_ref[...] + jnp.dot(p.astype(BF), vb.astype(BF), preferred_element_type=F32)
        m_ref[...] = m_new
        return p

    p = update(k_ref[...], v_ref[...], b_ref[...] if has_bias else None)

    @pl.when(j == last)
    def _():
        if has_new:
            update(kn_ref[...], vn_ref[...], bn_ref[...])
        inv = 1.0 / l_ref[...]
        o_ref[...] = acc_ref[...] * inv
        if emit_p:
            p_ref[...] = p * inv


def decode_attn(qbd, k, v, *, k_col, v_col, ck, cv, bias=None, new=None, emit_p=False, rk=1024):
    n_b, r = k.shape[0], k.shape[1]
    rk = min(rk, r)
    n_j = r // rk
    assert not emit_p or (n_j == 1 and new is None)
    in_specs = [pl.BlockSpec((None, LANES, ck), lambda b, j: (b, 0, 0)),
                pl.BlockSpec((None, rk, ck), lambda b, j: (b, j, k_col)),
                pl.BlockSpec((None, rk, cv), lambda b, j: (b, j, v_col))]
    args = [qbd, k, v]
    if bias is not None:
        in_specs.append(pl.BlockSpec((None, LANES, rk), lambda b, j: (b, 0, j)))
        args.append(bias)
    if new is not None:
        kn, vn, bn = new
        in_specs += [pl.BlockSpec((None, NEW_PAD, ck), lambda b, j: (b, 0, k_col)),
                     pl.BlockSpec((None, NEW_PAD, cv), lambda b, j: (b, 0, v_col)),
                     pl.BlockSpec((None, LANES, NEW_PAD), lambda b, j: (b, 0, 0))]
        args += [kn, vn, bn]
    out_specs = [pl.BlockSpec((None, LANES, cv), lambda b, j: (b, 0, 0))]
    out_shape = [jax.ShapeDtypeStruct((n_b, LANES, cv), F32)]
    if emit_p:
        out_specs.append(pl.BlockSpec((None, LANES, rk), lambda b, j: (b, 0, 0)))
        out_shape.append(jax.ShapeDtypeStruct((n_b, LANES, r), F32))
    kern = functools.partial(_decode_kernel, has_bias=bias is not None, has_new=new is not None, emit_p=emit_p)
    out = pl.pallas_call(
        kern,
        grid=(n_b, n_j),
        in_specs=in_specs,
        out_specs=out_specs,
        out_shape=out_shape,
        scratch_shapes=[pltpu.VMEM((LANES, 1), F32), pltpu.VMEM((LANES, 1), F32), pltpu.VMEM((LANES, cv), F32)],
        compiler_params=_cparams("parallel", "arbitrary"),
        name="decode_attn",
    )(*args)
    return out if emit_p else out[0]


def _paged_kernel(*refs, pp, n_pages, rpp, mode, blk_shift, has_new):
    it = iter(refs)
    pt_ref, q_ref = next(it), next(it)
    k_refs = [next(it) for _ in range(pp)]
    v_refs = [next(it) for _ in range(pp)]
    b_refs = sel_ref = f_refs = np_ref = None
    if mode == "dense":
        b_refs = [next(it) for _ in range(pp)]
    if mode == "sel":
        sel_ref = next(it)
    if mode == "fox":
        f_refs = [next(it) for _ in range(pp)]
        np_ref = next(it)
    if has_new:
        kn_ref, vn_ref, bn_ref = next(it), next(it), next(it)
    o_ref, m_ref, l_ref, acc_ref = next(it), next(it), next(it), next(it)
    carry_ref = next(it) if mode == "fox" else None
    step = pl.program_id(1)
    last = pl.num_programs(1) - 1

    @pl.when(step == 0)
    def _():
        m_ref[...] = jnp.full(m_ref.shape, NEG_INF, F32)
        l_ref[...] = jnp.zeros(l_ref.shape, F32)
        acc_ref[...] = jnp.zeros(acc_ref.shape, F32)
        if mode == "fox":
            carry_ref[...] = jnp.zeros(carry_ref.shape, F32)

    q = q_ref[...]
    if mode == "fox":
        n_h = f_refs[0].shape[0]
        lf = [f_refs[i][...] for i in range(pp)]
        tots = [jnp.sum(x, axis=-1, keepdims=True) for x in lf]
        carry = carry_ref[...]
        upper = (lax.broadcasted_iota(jnp.int32, (rpp, rpp), 0) >
                 lax.broadcasted_iota(jnp.int32, (rpp, rpp), 1)).astype(BF)
    scores = []
    for i in range(pp):
        s = jnp.dot(q, k_refs[i][...].astype(BF), preferred_element_type=F32)
        if mode == "dense":
            s = s + b_refs[i][...]
        if mode == "sel":
            pg = n_pages - 1 - (step * pp + i)
            key_blk = jnp.right_shift(pg * rpp + lax.broadcasted_iota(jnp.int32, (1, rpp), 1), blk_shift)
            expand = (lax.broadcasted_iota(jnp.int32, (LANES, rpp), 0) == key_blk).astype(BF)
            s = s + jnp.dot(sel_ref[...], expand, preferred_element_type=F32)
        if mode == "fox":
            x2 = jnp.concatenate([lf[i], lf[i]], axis=0)
            hi = x2.astype(BF)
            lo = (x2 - hi.astype(F32)).astype(BF)
            suffix = (jnp.dot(hi, upper, preferred_element_type=F32) +
                      jnp.dot(lo, upper, preferred_element_type=F32))[0:n_h]
            b8 = suffix + carry
            carry = carry + tots[i]
            s = s + (jnp.concatenate([b8] * (LANES // n_h), axis=0) + np_ref[...])
        scores.append(s)
    if mode == "fox":
        carry_ref[...] = carry

    def accumulate(scores, values_t):
        m_prev = m_ref[...]
        m_new = m_prev
        for s in scores:
            m_new = jnp.maximum(m_new, jnp.max(s, axis=-1, keepdims=True))
        alpha = jnp.exp(m_prev - m_new)
        l = alpha * l_ref[...]
        pv = None
        for s, vt in zip(scores, values_t):
            p = jnp.exp(s - m_new)
            l = l + jnp.sum(p, axis=-1, keepdims=True)
            t = vt(p.astype(BF))
            pv = t if pv is None else pv + t
        l_ref[...] = l
        acc_ref[...] = alpha * acc_ref[...] + pv
        m_ref[...] = m_new

    def pv_page(i):
        return lambda p: lax.dot_general(p, v_refs[i][...].astype(BF), (((1,), (1,)), ((), ())),
                                         preferred_element_type=F32)

    accumulate(scores, [pv_page(i) for i in range(pp)])

    @pl.when(step == last)
    def _():
        if has_new:
            s = lax.dot_general(q, kn_ref[...].astype(BF), (((1,), (1,)), ((), ())),
                                preferred_element_type=F32) + bn_ref[...]
            accumulate([s], [lambda p: jnp.dot(p, vn_ref[...].astype(BF), preferred_element_type=F32)])
        o_ref[...] = acc_ref[...] * (1.0 / l_ref[...])


def paged_decode(pt, qbd, cache_t, *, layer, k_slot, v_slot, pp=8, mode=None, bias=None, selbias=None,
                 blk=None, logf_t=None, newpre=None, new=None):
    n_b, n_pages = pt.shape
    c, rpp = cache_t.shape[3], cache_t.shape[4]
    pp = min(pp, n_pages)
    assert n_pages % pp == 0
    n_steps = n_pages // pp

    def page_map(b, s, pt_ref, *, i, slot):
        return (layer, pt_ref[b, n_pages - 1 - (s * pp + i)], slot, 0, 0)

    in_specs = [pl.BlockSpec((None, LANES, c), lambda b, s, pt_ref: (b, 0, 0))]
    args = [qbd]
    for slot in (k_slot, v_slot):
        for i in range(pp):
            in_specs.append(pl.BlockSpec((None, None, None, c, rpp), functools.partial(page_map, i=i, slot=slot)))
            args.append(cache_t)
    if mode == "dense":
        shared = bias.shape[0] == 1
        for i in range(pp):
            in_specs.append(pl.BlockSpec(
                (None, LANES, rpp),
                functools.partial(lambda b, s, pt_ref, *, i: (0 if shared else b, 0, n_pages - 1 - (s * pp + i)), i=i)))
            args.append(bias)
    if mode == "sel":
        in_specs.append(pl.BlockSpec((None, LANES, LANES), lambda b, s, pt_ref: (b, 0, 0)))
        args.append(selbias)
    if mode == "fox":
        n_h = logf_t.shape[2]
        for i in range(pp):
            in_specs.append(pl.BlockSpec(
                (None, None, n_h, rpp),
                functools.partial(lambda b, s, pt_ref, *, i: (layer, pt_ref[b, n_pages - 1 - (s * pp + i)], 0, 0), i=i)))
            args.append(logf_t)
        in_specs.append(pl.BlockSpec((None, LANES, LANES), lambda b, s, pt_ref: (b, 0, 0)))
        args.append(newpre)
    if new is not None:
        kn, vn, bn = new
        in_specs += [pl.BlockSpec((None, NEW_PAD, c), lambda b, s, pt_ref: (b, 0, 0)),
                     pl.BlockSpec((None, NEW_PAD, c), lambda b, s, pt_ref: (b, 0, 0)),
                     pl.BlockSpec((None, LANES, NEW_PAD), lambda b, s, pt_ref: (b, 0, 0))]
        args += [kn, vn, bn]
    scratch = [pltpu.VMEM((LANES, 1), F32), pltpu.VMEM((LANES, 1), F32), pltpu.VMEM((LANES, c), F32)]
    if mode == "fox":
        scratch.append(pltpu.VMEM((logf_t.shape[2], 1), F32))
    grid_spec = pltpu.PrefetchScalarGridSpec(
        num_scalar_prefetch=1,
        grid=(n_b, n_steps),
        in_specs=in_specs,
        out_specs=pl.BlockSpec((None, LANES, c), lambda b, s, pt_ref: (b, 0, 0)),
        scratch_shapes=scratch,
    )
    kern = functools.partial(_paged_kernel, pp=pp, n_pages=n_pages, rpp=rpp, mode=mode,
                             blk_shift=None if blk is None else int(math.log2(blk)), has_new=new is not None)
    return pl.pallas_call(
        kern,
        grid_spec=grid_spec,
        out_shape=jax.ShapeDtypeStruct((n_b, LANES, c), F32),
        compiler_params=_cparams("parallel", "arbitrary"),
        name="paged_" + (mode or "plain"),
    )(pt, *args)


def _block_sum_kernel(*refs, pp, ppb):
    pt_ref = refs[0]
    k_refs = refs[1:1 + pp]
    o_ref = refs[1 + pp]
    ones = jnp.ones((8, k_refs[0].shape[1]), BF)
    rows = []
    for bi in range(pp // ppb):
        acc = None
        for j in range(ppb):
            x = k_refs[bi * ppb + j][...]
            hi = x.astype(BF)
            lo = (x - hi.astype(F32)).astype(BF)
            t = (lax.dot_general(ones, hi, (((1,), (1,)), ((), ())), preferred_element_type=F32) +
                 lax.dot_general(ones, lo, (((1,), (1,)), ((), ())), preferred_element_type=F32))
            acc = t if acc is None else acc + t
        rows.append(acc[0:1])
    o_ref[...] = jnp.concatenate(rows, axis=0)


def page_block_sums(pt, cache_t, *, layer, slot, ppb, pp=16):
    n_b, n_pages = pt.shape
    c, rpp = cache_t.shape[3], cache_t.shape[4]
    pp = min(pp, n_pages)
    assert n_pages % pp == 0 and pp % ppb == 0
    in_specs = []
    for i in range(pp):
        in_specs.append(pl.BlockSpec(
            (None, None, None, c, rpp),
            functools.partial(lambda b, s, pt_ref, *, i: (layer, pt_ref[b, s * pp + i], slot, 0, 0), i=i)))
    grid_spec = pltpu.PrefetchScalarGridSpec(
        num_scalar_prefetch=1,
        grid=(n_b, n_pages // pp),
        in_specs=in_specs,
        out_specs=pl.BlockSpec((None, pp // ppb, c), lambda b, s, pt_ref: (b, s, 0)),
    )
    return pl.pallas_call(
        functools.partial(_block_sum_kernel, pp=pp, ppb=ppb),
        grid_spec=grid_spec,
        out_shape=jax.ShapeDtypeStruct((n_b, n_pages // ppb, c), F32),
        compiler_params=_cparams("parallel", "arbitrary"),
        name="page_block_sums",
    )(pt, *([cache_t] * pp))


def _merge_kernel(a_ref, b_ref, c_ref, d_ref, w_ref, g0_ref, g1_ref, g2_ref, g3_ref, o_ref):
    acc = None
    for n, (br, g) in enumerate(((a_ref, g0_ref), (b_ref, g1_ref), (c_ref, g2_ref), (d_ref, g3_ref))):
        y = jnp.dot(br[...].astype(BF), w_ref[n].astype(BF), preferred_element_type=F32)
        t = jax.nn.sigmoid(g[...].astype(F32)) * y
        acc = t if acc is None else acc + t
    o_ref[...] = acc.astype(o_ref.dtype)


def merge_branches(branches, w_branch, glog, d_model):
    n, bw = branches[0].shape
    tm = _tile(n, 640)
    tn = _tile(d_model, 512, LANES)
    nj = d_model // tn
    in_specs = [pl.BlockSpec((tm, bw), lambda i, j: (i, 0)) for _ in range(N_BRANCH)]
    in_specs.append(pl.BlockSpec((N_BRANCH, bw, tn), lambda i, j: (0, 0, j)))
    for nb in range(N_BRANCH):
        in_specs.append(pl.BlockSpec((tm, tn), functools.partial(lambda i, j, nb: (i, nb * nj + j), nb=nb)))
    return pl.pallas_call(
        _merge_kernel,
        grid=(n // tm, nj),
        in_specs=in_specs,
        out_specs=pl.BlockSpec((tm, tn), lambda i, j: (i, j)),
        out_shape=jax.ShapeDtypeStruct((n, d_model), BF),
        compiler_params=_cparams("parallel", "arbitrary"),
        name="merge",
    )(*branches, w_branch, glog, glog, glog, glog)


def _moe_kernel(be_ref, x_ref, wg_ref, wu_ref, wd_ref, o_ref):
    x = x_ref[...].astype(BF)
    g = jnp.dot(x, wg_ref[...].astype(BF), preferred_element_type=F32)
    u = jnp.dot(x, wu_ref[...].astype(BF), preferred_element_type=F32)
    hid = (jax.nn.silu(g) * u).astype(BF)
    o_ref[...] = jnp.dot(hid, wd_ref[...].astype(BF), preferred_element_type=F32)


def moe_experts(xs, blk_e, w_g, w_u, w_d):
    rows, d = xs.shape
    f = w_g.shape[2]
    n_blk = rows // MOE_BLOCK
    grid_spec = pltpu.PrefetchScalarGridSpec(
        num_scalar_prefetch=1,
        grid=(n_blk,),
        in_specs=[pl.BlockSpec((MOE_BLOCK, d), lambda i, be: (i, 0)),
                  pl.BlockSpec((None, d, f), lambda i, be: (be[i], 0, 0)),
                  pl.BlockSpec((None, d, f), lambda i, be: (be[i], 0, 0)),
                  pl.BlockSpec((None, f, d), lambda i, be: (be[i], 0, 0))],
        out_specs=pl.BlockSpec((MOE_BLOCK, d), lambda i, be: (i, 0)),
    )
    return pl.pallas_call(
        _moe_kernel,
        grid_spec=grid_spec,
        out_shape=jax.ShapeDtypeStruct((rows, d), F32),
        compiler_params=_cparams("arbitrary"),
        name="moe_experts",
    )(blk_e, xs, w_g, w_u, w_d)


def _rope_tables(pos):
    half = ROPE_DIMS // 2
    inv_freq = jnp.exp(jnp.arange(half, dtype=F32) * (-2.0 * math.log(ROPE_THETA) / ROPE_DIMS))
    ang = jnp.asarray(pos, F32)[:, None] * inv_freq[None, :]
    return jnp.cos(ang), jnp.sin(ang)


def _rope(x, cos, sin):
    half = ROPE_DIMS // 2
    c = cos[:, None, :]
    s = sin[:, None, :]
    x1, x2 = x[..., :half], x[..., half:ROPE_DIMS]
    return jnp.concatenate([x1 * c - x2 * s, x2 * c + x1 * s, x[..., ROPE_DIMS:]], axis=-1)


def _overlap_matrix(n_cmp, n_slc):
    st = np.arange(n_cmp) * CMP_STRIDE
    en = st + CMP_LEN
    j = np.arange(n_slc) * SEL_BLOCK
    return ((st[:, None] < j[None, :] + SEL_BLOCK) & (en[:, None] > j[None, :])).astype(np.float32)


def _block_expand_t(n_keys, block):
    return ((np.arange(n_keys) // block)[:, None] == np.arange(LANES)[None, :]).astype(np.float32)


def _nsa_select(imp, q_pos, n_slc):
    own = q_pos // SEL_BLOCK
    j = jnp.arange(n_slc, dtype=jnp.int32)[None, :]
    valid = j <= own[:, None]
    forced = (j == 0) | (j == own[:, None]) | (j == own[:, None] - 1)
    score = jnp.where(valid, imp + jnp.where(forced, SEL_BONUS, 0.0), -SEL_BONUS)
    n_top = min(N_SEL, n_slc)
    _, idx = lax.top_k(score, n_top)
    ok = idx <= own[None, None, :, None]
    onehot = (idx[..., None] == jnp.arange(n_slc, dtype=jnp.int32)) & ok[..., None]
    return jnp.any(onehot, axis=-2)


def _moba_select(sc, q_pos, n_blk):
    own = q_pos // MOBA_BLOCK
    past_blk = jnp.arange(n_blk, dtype=jnp.int32)[None, :] < own[:, None]
    sc = jnp.where(past_blk[None, None], sc, NEG_INF)
    n_top = min(MOBA_TOPK, n_blk)
    _, top = lax.top_k(sc, n_top)
    ok = top < own[None, None, :, None]
    onehot = (top[..., None] == jnp.arange(n_blk, dtype=jnp.int32)) & ok[..., None]
    return jnp.any(onehot, axis=-2)


def _pad_lanes(x, n=LANES):
    return jnp.pad(x, [(0, 0)] * (x.ndim - 1) + [(0, n - x.shape[-1])])


def _pad_rows(x, n=LANES):
    return jnp.pad(x, [(0, 0)] * (x.ndim - 2) + [(0, n - x.shape[-2]), (0, 0)])


def _compress(k, pe, w1, w2):
    b, tk, g, d = k.shape
    n_cmp = (tk - CMP_LEN) // CMP_STRIDE + 1
    idx = np.arange(n_cmp)[:, None] * CMP_STRIDE + np.arange(CMP_LEN)[None, :]
    blk = k[:, idx] + pe[None, None, :, None, :]
    blk = jnp.transpose(blk, (0, 1, 3, 2, 4)).reshape(b * n_cmp * g, CMP_LEN * d)
    return cmp_mlp(blk, w1, w2).reshape(b, n_cmp, g, d)


def _key_minor(cache):
    l, p, r, a, b, d = cache.shape
    return jnp.transpose(cache, (0, 1, 3, 4, 5, 2)).reshape(l, p, a, b * d, r)


def _qbd(q, n_heads, d_head, scale):
    b, tq, c = q.shape
    qh = q.reshape(b, tq, n_heads, d_head) * scale
    eye = jnp.eye(n_heads, dtype=q.dtype)
    bd = jnp.einsum("bthd,hg->bthgd", qh, eye).reshape(b, tq * n_heads, c)
    return _pad_rows(bd).astype(BF)


def _diag_blocks(o, n_heads, tq, d_out):
    b = o.shape[0]
    o = o[:, :tq * n_heads].reshape(b, tq, n_heads, n_heads, d_out)
    idx = jnp.arange(n_heads)
    return o[:, :, idx, idx]


def _group_blocks(o, tq, hpg):
    b = o.shape[0]
    o = o[:, :tq * G_NSA * hpg].reshape(b, tq, G_NSA, hpg, G_NSA, HEAD_DIM)
    gi = jnp.arange(G_NSA)
    o = o[:, :, gi, :, gi]
    return jnp.transpose(o, (1, 2, 0, 3, 4)).reshape(b, tq, G_NSA * hpg * HEAD_DIM)


def _query_rows(x):
    b, t, h, r = x.shape
    return _pad_rows(x.reshape(b, t * h, r))


def kernel(x_prompt, x_sample, cache_diff_kv, cache_fox_kv, cache_fox_logf, cache_nsa_kv, cache_nsa_win,
           cache_moba_kv, cache_mem_kv, page_table, mem_prompt, norm_mix_g, w_in, b_forget, diff_lambda,
           diff_subln_g, nsa_cmp_pe, nsa_cmp_w1, nsa_cmp_w2, w_branch, w_out, norm_mem_g, norm_memkv_g,
           w_mem_q, w_mem_kv, w_mem_o, norm_ffn_g, w_router_group, b_router_group, w_router_expert,
           b_router_expert, w_exp_gate, w_exp_up, w_exp_down, norm_final_g):
    depth = w_in.shape[0]
    bp, t, d_model = x_prompt.shape
    bs, ts, _ = x_sample.shape
    n_pool, page = cache_diff_kv.shape[1], cache_diff_kv.shape[2]
    n_pages = page_table.shape[1]
    past_len = n_pages * page
    bw = d_model // N_BRANCH
    h_diff = bw // (2 * HEAD_DIM)
    h8 = bw // HEAD_DIM
    hpg = h8 // G_NSA
    gw = G_NSA * HEAD_DIM
    n_mem = mem_prompt.shape[1]
    mem_hd = bw // H_MEM
    n_exp = N_GROUPS * EXPERTS_PER_GROUP
    np_tok = bp * t
    ns_tok = bs * ts
    n_tok = np_tok + ns_tok
    assert h8 * ts <= LANES and ts < CMP_STRIDE and past_len % MOBA_BLOCK == 0 and page % CMP_STRIDE == 0
    assert past_len // SEL_BLOCK <= LANES and LANES % h8 == 0
    scale = HEAD_DIM ** -0.5
    page_table = page_table.astype(jnp.int32)

    pos_p = np.arange(t, dtype=np.int32)
    pos_s = past_len + np.arange(ts, dtype=np.int32)
    pos_all = np.concatenate([np.tile(pos_p, bp), np.tile(pos_s, bs)])
    cos, sin = _rope_tables(pos_all)

    sizes = (bw, bw, bw, bw, bw, bw, h8, bw, 6 * G_NSA * HEAD_DIM, 3 * h8, bw, bw, bw, N_BRANCH * d_model)
    offs = np.concatenate([[0], np.cumsum(sizes)])
    seg = lambda i: (int(offs[i]), int(offs[i + 1]))
    main_ids = (0, 1, 2, 3, 4, 5, 7, 8, 10, 11, 12)
    small_ids = (6, 9)
    n_main = sum(sizes[i] for i in main_ids)
    n_small = sum(sizes[i] for i in small_ids)
    n_main_pad = -(-(n_main + n_small) // 512) * 512
    mo = {}
    o = 0
    for i in main_ids + small_ids:
        mo[i] = o
        o += sizes[i]

    x = jnp.concatenate([x_prompt.reshape(np_tok, d_model), x_sample.reshape(ns_tok, d_model)], axis=0)

    e_moba = jnp.asarray(_block_expand_t(t, MOBA_BLOCK), BF)
    e_nsa = jnp.asarray(_block_expand_t(t, SEL_BLOCK), BF)
    n_cmp_p = (t - CMP_LEN) // CMP_STRIDE + 1
    n_cmp_p_pad = -(-n_cmp_p // LANES) * LANES
    n_slc_p = -(-t // SEL_BLOCK)
    ov_p = np.zeros((n_cmp_p_pad, LANES), np.float32)
    ov_p[:n_cmp_p, :n_slc_p] = _overlap_matrix(n_cmp_p, n_slc_p)
    ov_p = jnp.asarray(ov_p, BF)
    n_mblk_p = -(-t // MOBA_BLOCK)

    tk_s = past_len + ts
    n_cmp_s = (tk_s - CMP_LEN) // CMP_STRIDE + 1
    n_cmp_s_pad = -(-n_cmp_s // LANES) * LANES
    n_slc_s = -(-tk_s // SEL_BLOCK)
    n_slc_past = past_len // SEL_BLOCK
    ov_s = jnp.asarray(_overlap_matrix(n_cmp_s, n_slc_s))
    n_mblk_s = -(-tk_s // MOBA_BLOCK)
    n_mblk_past = past_len // MOBA_BLOCK
    row_t = np.minimum(np.arange(LANES) // h8, ts - 1)
    used = np.arange(LANES) < h8 * ts

    def new_bias(n_heads):
        rt = np.minimum(np.arange(LANES) // n_heads, ts - 1)
        keyi = np.arange(NEW_PAD)
        okm = (keyi[None, :] <= rt[:, None]) & (keyi[None, :] < ts)
        return np.where(okm, 0.0, NEG_INF).astype(np.float32)

    def new_rows(a):
        return _pad_rows(a.reshape(bs, ts, a.shape[-1]), NEW_PAD)

    diff_t = _key_minor(cache_diff_kv)
    fox_t = _key_minor(cache_fox_kv)
    moba_t = _key_minor(cache_moba_kv)
    nsa_t = _key_minor(cache_nsa_kv)
    win_t = _key_minor(cache_nsa_win)
    mem_t = _key_minor(cache_mem_kv)
    logf_t = jnp.transpose(cache_fox_logf, (0, 1, 3, 2))
    wb = cache_nsa_win.shape[2]
    own_pt = jnp.arange(bs, dtype=jnp.int32).reshape(bs, 1)
    cpp = page // CMP_STRIDE
    nsa_chunks = cache_nsa_kv[:, :, :, 0:2].reshape(depth, n_pool, cpp, CMP_STRIDE, 2, G_NSA, HEAD_DIM)
    nsa_chunks = jnp.transpose(nsa_chunks, (0, 4, 1, 2, 5, 3, 6)).reshape(
        depth, 2, n_pool * cpp * G_NSA, CMP_STRIDE * HEAD_DIM)

    win_ok = (np.arange(wb)[None, :] >= row_t[:, None] + (wb - WINDOW)) | ~used[:, None]
    win_bias = jnp.asarray(np.where(win_ok, 0.0, NEG_INF).astype(np.float32)).reshape(1, LANES, wb)
    ends_s = np.arange(n_cmp_s_pad) * CMP_STRIDE + (CMP_LEN - 1)
    cmp_ok = (ends_s[None, :] <= (past_len + row_t)[:, None]) & (np.arange(n_cmp_s_pad)[None, :] < n_cmp_s)
    cmp_bias = jnp.broadcast_to(jnp.asarray(np.where(cmp_ok, 0.0, NEG_INF).astype(np.float32)),
                                (bs, LANES, n_cmp_s_pad))

    acc_p = [[] for _ in range(7)]
    acc_s = [[] for _ in range(6)]

    for l in range(depth):
        lam_init = 0.8 - 0.6 * math.exp(-0.3 * l)
        lam_p = diff_lambda[l].astype(F32)
        lam = jnp.exp(jnp.sum(lam_p[0] * lam_p[1])) - jnp.exp(jnp.sum(lam_p[2] * lam_p[3])) + lam_init

        wl = w_in[l]
        w_main = jnp.concatenate([wl[:, seg(i)[0]:seg(i)[1]] for i in main_ids + small_ids], axis=1)
        w_main = jnp.pad(w_main, ((0, 0), (0, n_main_pad - n_main - n_small))).astype(BF)
        w_gate = wl[:, seg(13)[0]:seg(13)[1]].astype(BF)
        h = rmsnorm(x, norm_mix_g[l], BF)
        proj = matmul(h, w_main)
        glog = matmul(h, w_gate, tn_target=1024)

        def col(i):
            return proj[:, mo[i]:mo[i] + sizes[i]]

        dq = _rope(col(0).reshape(n_tok, 2 * h_diff, HEAD_DIM), cos, sin).reshape(n_tok, bw)
        dk = _rope(col(1).reshape(n_tok, 2 * h_diff, HEAD_DIM), cos, sin).reshape(n_tok, bw)
        diff_new = jnp.concatenate([dk, col(2)], axis=1)
        fq = col(3)
        fox_new = jnp.concatenate([col(4), col(5)], axis=1)
        logf_new = jax.nn.log_sigmoid(col(6).astype(F32) + b_forget[l].astype(F32))
        nq = _rope(col(7).reshape(n_tok, h8, HEAD_DIM), cos, sin).reshape(n_tok, bw)
        nkv = col(8).reshape(n_tok, 6, G_NSA, HEAD_DIM)
        nsa_new = jnp.stack([nkv[:, 0], nkv[:, 1], _rope(nkv[:, 2], cos, sin), nkv[:, 3]], axis=1)
        win_new = jnp.stack([_rope(nkv[:, 4], cos, sin), nkv[:, 5]], axis=1)
        ngate = jax.nn.sigmoid(col(9).astype(F32)).reshape(n_tok, h8, 3)
        mq = _rope(col(10).reshape(n_tok, h8, HEAD_DIM), cos, sin).reshape(n_tok, bw)
        mk = _rope(col(11).reshape(n_tok, h8, HEAD_DIM), cos, sin).reshape(n_tok, bw)
        moba_new = jnp.concatenate([mk, col(12)], axis=1)
        nsa_flat = nsa_new.reshape(n_tok, 4 * gw)
        win_flat = win_new.reshape(n_tok, 2 * gw)

        P = slice(0, np_tok)

        def keys_bf(a):
            return a.astype(BF)

        def vals_t(a):
            return jnp.transpose(a.reshape(bp, t, a.shape[-1]), (0, 2, 1)).astype(BF)

        o_a_p = flash_prompt(dq[P], keys_bf(dk[P]), vals_t(col(2)[P]), n_batch=bp, t_q=t, t_k=t, n_cols=h_diff,
                             q_col=lambda c: c, k_col=lambda c: c, v_col=lambda c: c, mode="diff",
                             scale=scale, lam=lam, subln_g=diff_subln_g[l], lam_init=lam_init)
        logf_p = logf_new[P].reshape(bp, t, h8)
        c_all = jnp.cumsum(logf_p, axis=1).reshape(bp, t, h8 // 2, 2)
        cq = jnp.transpose(c_all, (0, 2, 3, 1))
        ck = jnp.transpose(c_all, (0, 2, 1, 3))
        o_b_p = flash_prompt(fq[P], keys_bf(col(4)[P]), vals_t(col(5)[P]), n_batch=bp, t_q=t, t_k=t, n_cols=h8 // 2,
                             q_col=lambda c: c, k_col=lambda c: c, v_col=lambda c: c, mode="pair",
                             scale=scale, bias=(cq, ck))
        nsa_p = nsa_new[P].reshape(bp, t, 4, G_NSA, HEAD_DIM)
        k_cmp = _compress(nsa_p[:, :, 0], nsa_cmp_pe[l, 0], nsa_cmp_w1[l, 0], nsa_cmp_w2[l, 0])
        v_cmp = _compress(nsa_p[:, :, 1], nsa_cmp_pe[l, 1], nsa_cmp_w1[l, 1], nsa_cmp_w2[l, 1])

        def dup_cmp(a):
            a = jnp.transpose(a, (0, 2, 1, 3))
            a = jnp.concatenate([a, a], axis=-1)
            return jnp.pad(a, ((0, 0), (0, 0), (0, n_cmp_p_pad - n_cmp_p), (0, 0)))

        o_cmp_p, imp_p = cmp_attn_prompt(nq[P], dup_cmp(k_cmp), dup_cmp(v_cmp), ov_p, n_batch=bp, t_q=t)
        sel_nsa = _nsa_select(imp_p[..., :n_slc_p], jnp.asarray(pos_p), n_slc_p)
        sel_nsa = _pad_rows(jnp.transpose(sel_nsa, (0, 1, 3, 2)).astype(BF))

        def dup_g(a):
            return jnp.concatenate([a, a], axis=-1).reshape(a.shape[0], G_NSA * LANES)

        nsa_pt = nsa_new[P]
        ppg = hpg // 2
        o_sel_p = flash_prompt(nq[P], keys_bf(dup_g(nsa_pt[:, 2])), vals_t(dup_g(nsa_pt[:, 3])), n_batch=bp, t_q=t,
                               t_k=t, n_cols=h8 // 2, q_col=lambda c: c, k_col=lambda c: c // ppg,
                               v_col=lambda c: c // ppg, mode="pair", scale=scale, sel=sel_nsa,
                               sel_mode="per_group", emat=e_nsa)
        win_p = win_new[P]
        o_win_p = flash_prompt(nq[P], keys_bf(dup_g(win_p[:, 0])), vals_t(dup_g(win_p[:, 1])), n_batch=bp, t_q=t,
                               t_k=t, n_cols=h8 // 2, q_col=lambda c: c, k_col=lambda c: c // ppg,
                               v_col=lambda c: c // ppg, mode="pair", scale=scale, window=True)
        mk_p = mk[P].reshape(bp, n_mblk_p, MOBA_BLOCK, h8, HEAD_DIM)
        k_mean_p = jnp.transpose(jnp.mean(mk_p.astype(F32), axis=2), (0, 2, 1, 3))
        sc_p = jnp.einsum("bqhd,bhnd->bhqn", mq[P].reshape(bp, t, h8, HEAD_DIM).astype(F32), k_mean_p)
        sel_moba = _moba_select(sc_p, jnp.asarray(pos_p), n_mblk_p)
        own_p = jnp.asarray(pos_p // MOBA_BLOCK)
        sel_moba = sel_moba | (jnp.arange(n_mblk_p)[None, :] == own_p[:, None])[None, None]
        sel_moba = _pad_rows(jnp.transpose(sel_moba, (0, 1, 3, 2)).astype(BF))
        o_d_p = flash_prompt(mq[P], keys_bf(mk[P]), vals_t(col(12)[P]), n_batch=bp, t_q=t, t_k=t, n_cols=h8 // 2,
                             q_col=lambda c: c, k_col=lambda c: c, v_col=lambda c: c, mode="pair",
                             scale=scale, sel=sel_moba, sel_mode="per_head", emat=e_moba)

        S = slice(np_tok, n_tok)
        nb8 = jnp.broadcast_to(jnp.asarray(new_bias(h8)), (bs, LANES, NEW_PAD))
        qbd = _qbd(dq[S].reshape(bs, ts, bw), 2 * h_diff, HEAD_DIM, scale)
        dnew = new_rows(diff_new[S])
        o = paged_decode(page_table, qbd, diff_t, layer=l, k_slot=0, v_slot=1,
                         new=(dnew[..., :bw], dnew[..., bw:], nb8))
        o = o[:, :ts * 2 * h_diff].reshape(bs, ts, h_diff, 2, h_diff, 2 * HEAD_DIM)
        hi_ = jnp.arange(h_diff)
        o = o[:, :, hi_, :, hi_]
        o = jnp.transpose(o[:, :, :, 0] - lam * o[:, :, :, 1], (1, 2, 0, 3))
        y = o * lax.rsqrt(jnp.mean(o * o, axis=-1, keepdims=True) + NORM_EPS)
        o_a_s = ((y * diff_subln_g[l].astype(F32)) * (1.0 - lam_init)).reshape(ns_tok, bw)

        logf_s = logf_new[S].reshape(bs, ts, h8)
        pre_new = jnp.cumsum(logf_s, axis=1)
        newpre = jnp.broadcast_to(_query_rows(pre_new[..., None]), (bs, LANES, LANES))
        d_new = pre_new[:, :, None, :] - pre_new[:, None, :, :]
        d_new = _pad_lanes(_query_rows(jnp.transpose(d_new, (0, 1, 3, 2))), NEW_PAD)
        fnew = new_rows(fox_new[S])
        qbd = _qbd(fq[S].reshape(bs, ts, bw), h8, HEAD_DIM, scale)
        o = paged_decode(page_table, qbd, fox_t, layer=l, k_slot=0, v_slot=1, mode="fox", logf_t=logf_t,
                         newpre=newpre, new=(fnew[..., :bw], fnew[..., bw:], d_new + nb8))
        o_b_s = _diag_blocks(o, h8, ts, HEAD_DIM).reshape(ns_tok, bw)

        kv_cmp_s = []
        for w in range(2):
            w1 = nsa_cmp_w1[l, w]
            half = CMP_STRIDE * HEAD_DIM
            w_ab = jnp.concatenate([w1[:half], w1[half:]], axis=1)
            ab = matmul(nsa_chunks[l, w], w_ab, tm_target=1024)
            hid = w1.shape[1]
            ab = ab.reshape(n_pool, cpp, G_NSA, 2 * hid)[page_table].reshape(bs, n_pages * cpp, G_NSA, 2 * hid)
            pre = ab[:, :n_cmp_s, :, :hid] + ab[:, 1:n_cmp_s + 1, :, hid:]
            kv = cmp_finish(pre.reshape(bs * n_cmp_s * G_NSA, hid), nsa_cmp_pe[l, w].reshape(-1), w1,
                            nsa_cmp_w2[l, w])
            kv_cmp_s.append(kv.reshape(bs, n_cmp_s, gw))
        kv_cmp_s = jnp.pad(jnp.concatenate(kv_cmp_s, axis=-1), ((0, 0), (0, n_cmp_s_pad - n_cmp_s), (0, 0)))
        qbd_n = _qbd(nq[S].reshape(bs, ts, bw), h8, HEAD_DIM, scale)
        qbd_g = qbd_n.reshape(bs, LANES, G_NSA, hpg, HEAD_DIM).sum(axis=3).reshape(bs, LANES, gw)
        o, p_c = decode_attn(qbd_g, kv_cmp_s, kv_cmp_s, k_col=0, v_col=1, ck=gw, cv=gw, bias=cmp_bias,
                             emit_p=True, rk=n_cmp_s_pad)
        o_cmp_s = _group_blocks(o, ts, hpg).reshape(ns_tok, bw)
        p_c = p_c[:, :h8 * ts, :n_cmp_s].reshape(bs, ts, G_NSA, hpg, n_cmp_s)
        imp_s = jnp.einsum("bqgmn,ns->bgqs", p_c, ov_s)
        sel_s = _nsa_select(imp_s, jnp.asarray(pos_s), n_slc_s)
        sel_rows = jnp.broadcast_to(jnp.transpose(sel_s, (0, 2, 1, 3))[:, :, :, None, :n_slc_past],
                                    (bs, ts, G_NSA, hpg, n_slc_past)).reshape(bs, ts, h8, n_slc_past)
        selbias = _pad_lanes(jnp.where(_query_rows(sel_rows.astype(F32)) > 0.5, 0.0, NEG_INF))
        selbias = jnp.where(jnp.asarray(used)[None, :, None], selbias, 0.0).astype(BF)
        nnew = new_rows(nsa_flat[S])
        o = paged_decode(page_table, qbd_g, nsa_t, layer=l, k_slot=2, v_slot=3, mode="sel", selbias=selbias,
                         blk=SEL_BLOCK, new=(nnew[..., 2 * gw:3 * gw], nnew[..., 3 * gw:], nb8))
        o_sel_s = _group_blocks(o, ts, hpg).reshape(ns_tok, bw)
        wnew = new_rows(win_flat[S])
        o = paged_decode(own_pt, qbd_g, win_t, layer=l, k_slot=0, v_slot=1, mode="dense", bias=win_bias,
                         new=(wnew[..., :gw], wnew[..., gw:], nb8))
        o_win_s = _group_blocks(o, ts, hpg).reshape(ns_tok, bw)
        wbuf = cache_nsa_win[l].reshape(bs, wb, 2 * gw)
        win_state = jnp.concatenate([wbuf, win_flat[S].reshape(bs, ts, 2 * gw)], axis=1)[:, ts:]

        k_sum = page_block_sums(page_table, moba_t, layer=l, slot=0, ppb=MOBA_BLOCK // page)
        k_mean_s = (k_sum / MOBA_BLOCK).reshape(bs, n_mblk_past, h8, HEAD_DIM)
        k_mean_s = jnp.pad(jnp.transpose(k_mean_s, (0, 2, 1, 3)), ((0, 0), (0, 0), (0, n_mblk_s - n_mblk_past), (0, 0)))
        sc_s = jnp.einsum("bqhd,bhnd->bhqn", mq[S].reshape(bs, ts, h8, HEAD_DIM).astype(F32), k_mean_s)
        selm_s = _moba_select(sc_s, jnp.asarray(pos_s), n_mblk_s)
        selm_rows = _query_rows(jnp.transpose(selm_s[..., :n_mblk_past], (0, 2, 1, 3)).astype(F32))
        mbias = _pad_lanes(jnp.where(selm_rows > 0.5, 0.0, NEG_INF))
        mbias = jnp.where(jnp.asarray(used)[None, :, None], mbias, 0.0).astype(BF)
        mnew = new_rows(moba_new[S])
        qbd = _qbd(mq[S].reshape(bs, ts, bw), h8, HEAD_DIM, scale)
        o = paged_decode(page_table, qbd, moba_t, layer=l, k_slot=0, v_slot=1, mode="sel", selbias=mbias,
                         blk=MOBA_BLOCK, new=(mnew[..., :bw], mnew[..., bw:], nb8))
        o_d_s = _diag_blocks(o, h8, ts, HEAD_DIM).reshape(ns_tok, bw)

        o_a = jnp.concatenate([o_a_p, o_a_s], axis=0)
        o_b = jnp.concatenate([o_b_p, o_b_s], axis=0)
        o_cmp = jnp.concatenate([o_cmp_p, o_cmp_s], axis=0).reshape(n_tok, h8, HEAD_DIM)
        o_sel = jnp.concatenate([o_sel_p, o_sel_s], axis=0).reshape(n_tok, h8, HEAD_DIM)
        o_win = jnp.concatenate([o_win_p, o_win_s], axis=0).reshape(n_tok, h8, HEAD_DIM)
        o_c = (ngate[..., 0:1] * o_cmp + ngate[..., 1:2] * o_sel + ngate[..., 2:3] * o_win).reshape(n_tok, bw)
        o_d = jnp.concatenate([o_d_p, o_d_s], axis=0)
        merged = merge_branches([o_a, o_b, o_c, o_d], w_branch[l], glog, d_model)
        x = matmul(merged, w_out[l], res=x)

        mem_h = rmsnorm(mem_prompt.reshape(bp * n_mem, d_model), norm_memkv_g[l], BF)
        mem_kv_p = matmul(mem_h, w_mem_kv[l])
        hm = rmsnorm(x, norm_mem_g[l], BF)
        qm = matmul(hm, w_mem_q[l])
        mem_vt = jnp.transpose(mem_kv_p[:, bw:].reshape(bp, n_mem, bw), (0, 2, 1)).astype(BF)
        o_m_p = flash_prompt(qm[P], mem_kv_p[:, :bw].astype(BF), mem_vt, n_batch=bp, t_q=t, t_k=n_mem,
                             n_cols=H_MEM, q_col=lambda c: c, k_col=lambda c: c, v_col=lambda c: c, mode="full",
                             scale=mem_hd ** -0.5, causal=False)
        qbd = _qbd(qm[S].reshape(bs, ts, bw), H_MEM, mem_hd, mem_hd ** -0.5)
        o = paged_decode(own_pt, qbd, mem_t, layer=l, k_slot=0, v_slot=1)
        o_m_s = _diag_blocks(o, H_MEM, ts, mem_hd).reshape(ns_tok, bw)
        x = matmul(jnp.concatenate([o_m_p, o_m_s], axis=0), w_mem_o[l], res=x)

        hf = rmsnorm(x, norm_ffn_g[l], BF)
        w_r = _pad_lanes(jnp.concatenate([w_router_group[l], w_router_expert[l]], axis=1))
        logits = matmul(hf, w_r, tn_target=LANES)
        rows = jnp.arange(n_tok)
        lg = logits[:, :N_GROUPS] + b_router_group[l].astype(F32)
        grp = jnp.argmax(lg, axis=-1).astype(jnp.int32)
        p_grp = jax.nn.softmax(lg, axis=-1)[rows, grp]
        le = (logits[:, N_GROUPS:N_GROUPS + n_exp] + b_router_expert[l].astype(F32)).reshape(n_tok, N_GROUPS, EXPERTS_PER_GROUP)
        p_exp = jax.nn.softmax(le[rows, grp], axis=-1)
        w_top, i_top = lax.top_k(p_exp, TOPK_IN_GROUP)
        gate = p_grp[:, None] * w_top / jnp.sum(w_top, axis=-1, keepdims=True)
        eid = grp[:, None] * EXPERTS_PER_GROUP + i_top.astype(jnp.int32)
        kk = TOPK_IN_GROUP
        m_asg = n_tok * kk
        flat_e = eid.reshape(m_asg)
        order = jnp.argsort(flat_e)
        se = flat_e[order]
        counts = jnp.bincount(flat_e, length=n_exp).astype(jnp.int32)
        padded = ((counts + MOE_BLOCK - 1) // MOE_BLOCK) * MOE_BLOCK
        pad_end = jnp.cumsum(padded)
        pad_start = pad_end - padded
        start = jnp.cumsum(counts) - counts
        dest = pad_start[se] + (jnp.arange(m_asg, dtype=jnp.int32) - start[se])
        n_blk = -(-(m_asg + n_exp * (MOE_BLOCK - 1)) // MOE_BLOCK)
        tok = order // kk
        xs = jnp.zeros((n_blk * MOE_BLOCK, d_model), BF).at[dest].set(hf[tok])
        blk_e = jnp.minimum(jnp.searchsorted(pad_end, jnp.arange(n_blk, dtype=jnp.int32) * MOE_BLOCK, side="right"),
                            n_exp - 1).astype(jnp.int32)
        ys = moe_experts(xs, blk_e, w_exp_gate[l], w_exp_up[l], w_exp_down[l])
        dest_flat = jnp.zeros((m_asg,), jnp.int32).at[order].set(dest)
        contrib = ys[dest_flat].reshape(n_tok, kk, d_model) * gate[..., None].astype(F32)
        x = x + jnp.sum(contrib, axis=1)

        def pr(a, shape):
            return a[P].reshape((bp, t) + shape)

        def sr(a, shape):
            return a[S].reshape((bs, ts) + shape)

        acc_p[0].append(pr(diff_new, (2, h_diff, 2 * HEAD_DIM)))
        acc_p[1].append(pr(fox_new, (2, h8, HEAD_DIM)))
        acc_p[2].append(pr(logf_new, (h8,)))
        acc_p[3].append(pr(nsa_flat, (4, G_NSA, HEAD_DIM)))
        acc_p[4].append(pr(win_flat, (2, G_NSA, HEAD_DIM))[:, t - min(WINDOW, t):])
        acc_p[5].append(pr(moba_new, (2, h8, HEAD_DIM)))
        acc_p[6].append(mem_kv_p.reshape(bp, n_mem, 2, H_MEM, mem_hd))
        acc_s[0].append(sr(diff_new, (2, h_diff, 2 * HEAD_DIM)))
        acc_s[1].append(sr(fox_new, (2, h8, HEAD_DIM)))
        acc_s[2].append(sr(logf_new, (h8,)))
        acc_s[3].append(sr(nsa_flat, (4, G_NSA, HEAD_DIM)))
        acc_s[4].append(win_state.reshape(bs, wb, 2, G_NSA, HEAD_DIM))
        acc_s[5].append(sr(moba_new, (2, h8, HEAD_DIM)))

    y = rmsnorm(x, norm_final_g, F32)
    y_prompt = y[:np_tok].reshape(bp, t, d_model)
    y_sample = y[np_tok:].reshape(bs, ts, d_model)
    return (y_prompt, y_sample) + tuple(jnp.stack(a) for a in acc_p) + tuple(jnp.stack(a) for a in acc_s)
```

```python
import functools
import math

import numpy as np
import jax
import jax.numpy as jnp
from jax import lax
from jax.experimental import pallas as pl
from jax.experimental.pallas import tpu as pltpu

F32 = jnp.float32
BF = jnp.bfloat16

HEAD_DIM = 64
ROPE_DIMS = HEAD_DIM // 4
ROPE_THETA = 500000.0
N_BRANCH = 4
NORM_EPS = 1e-6
NEG_INF = -1e30
G_NSA = 2
CMP_LEN = 32
CMP_STRIDE = 16
SEL_BLOCK = 64
N_SEL = 16
WINDOW = 512
SEL_BONUS = 1e4
MOBA_BLOCK = 256
MOBA_TOPK = 3
H_MEM = 4
N_GROUPS = 4
EXPERTS_PER_GROUP = 8
TOPK_IN_GROUP = 2
MOE_BLOCK = 128
LANES = 128
NEW_PAD = 128
DEC_ROWS = 32
LOG2E = 1.4426950408889634

VMEM_LIMIT = 56 * 1024 * 1024


def _cparams(*sem):
    return pltpu.CompilerParams(dimension_semantics=sem, vmem_limit_bytes=VMEM_LIMIT)


def _tile(n, target, mult=8):
    best = None
    for t in range(mult, min(n, target) + 1, mult):
        if n % t == 0:
            best = t
    return best if best is not None else n


def _rms_kernel(x_ref, g_ref, o_ref):
    x = x_ref[...].astype(F32)
    r = lax.rsqrt(jnp.mean(x * x, axis=-1, keepdims=True) + NORM_EPS)
    o_ref[...] = ((x * r) * g_ref[...]).astype(o_ref.dtype)


def rmsnorm(x, g, dtype):
    n, d = x.shape
    tm = _tile(n, 512)
    return pl.pallas_call(
        _rms_kernel,
        grid=(n // tm,),
        in_specs=[pl.BlockSpec((tm, d), lambda i: (i, 0)), pl.BlockSpec((1, d), lambda i: (0, 0))],
        out_specs=pl.BlockSpec((tm, d), lambda i: (i, 0)),
        out_shape=jax.ShapeDtypeStruct((n, d), dtype),
        compiler_params=_cparams("parallel"),
        name="rmsnorm",
    )(x, g.reshape(1, d).astype(F32))


def _mm_kernel(a_ref, w_ref, o_ref):
    o_ref[...] = jnp.dot(a_ref[...].astype(BF), w_ref[...].astype(BF),
                         preferred_element_type=F32).astype(o_ref.dtype)


def _mm_res_kernel(a_ref, w_ref, r_ref, o_ref):
    y = jnp.dot(a_ref[...].astype(BF), w_ref[...].astype(BF), preferred_element_type=F32)
    o_ref[...] = (r_ref[...] + y).astype(o_ref.dtype)


def matmul(a, w, res=None, out_dtype=F32, tm_target=640, tn_target=512, layer=None):
    m, k = a.shape
    n = w.shape[-1]
    tm = _tile(m, tm_target)
    tn = _tile(n, tn_target, LANES)
    if layer is None:
        w_spec = pl.BlockSpec((k, tn), lambda i, j: (0, j))
    else:
        w_spec = pl.BlockSpec((None, k, tn), lambda i, j: (layer, 0, j))
    in_specs = [pl.BlockSpec((tm, k), lambda i, j: (i, 0)), w_spec]
    args = [a, w]
    kern = _mm_kernel
    if res is not None:
        in_specs.append(pl.BlockSpec((tm, tn), lambda i, j: (i, j)))
        args.append(res)
        kern = _mm_res_kernel
    return pl.pallas_call(
        kern,
        grid=(m // tm, n // tn),
        in_specs=in_specs,
        out_specs=pl.BlockSpec((tm, tn), lambda i, j: (i, j)),
        out_shape=jax.ShapeDtypeStruct((m, n), out_dtype),
        compiler_params=_cparams("parallel", "arbitrary"),
        name="matmul",
    )(*args)


def _cmp_mlp_kernel(x_ref, w1_ref, w2_ref, o_ref):
    h = jnp.dot(x_ref[...].astype(BF), w1_ref[...].astype(BF), preferred_element_type=F32)
    h = jax.nn.gelu(h)
    o_ref[...] = jnp.dot(h.astype(BF), w2_ref[...].astype(BF), preferred_element_type=F32)


def cmp_mlp(x, w1, w2):
    r, k = x.shape
    f = w1.shape[1]
    d = w2.shape[1]
    tm = _tile(r, 512)
    return pl.pallas_call(
        _cmp_mlp_kernel,
        grid=(r // tm,),
        in_specs=[pl.BlockSpec((tm, k), lambda i: (i, 0)), pl.BlockSpec((k, f), lambda i: (0, 0)),
                  pl.BlockSpec((f, d), lambda i: (0, 0))],
        out_specs=pl.BlockSpec((tm, d), lambda i: (i, 0)),
        out_shape=jax.ShapeDtypeStruct((r, d), F32),
        compiler_params=_cparams("parallel"),
        name="cmp_mlp",
    )(x, w1, w2)


def _cmp_finish_kernel(x_ref, pe_ref, w1_ref, w2_ref, o_ref):
    pe_term = jnp.dot(pe_ref[...].astype(BF), w1_ref[...].astype(BF), preferred_element_type=F32)[0:1]
    h = jax.nn.gelu(x_ref[...] + pe_term)
    o_ref[...] = jnp.dot(h.astype(BF), w2_ref[...].astype(BF), preferred_element_type=F32)


def cmp_finish(x, pe_flat, w1, w2):
    r, f = x.shape
    k = w1.shape[0]
    d = w2.shape[1]
    tm = _tile(r, 1024)
    pe8 = jnp.broadcast_to(pe_flat.reshape(1, k), (8, k))
    return pl.pallas_call(
        _cmp_finish_kernel,
        grid=(r // tm,),
        in_specs=[pl.BlockSpec((tm, f), lambda i: (i, 0)), pl.BlockSpec((8, k), lambda i: (0, 0)),
                  pl.BlockSpec((k, f), lambda i: (0, 0)), pl.BlockSpec((f, d), lambda i: (0, 0))],
        out_specs=pl.BlockSpec((tm, d), lambda i: (i, 0)),
        out_shape=jax.ShapeDtypeStruct((r, d), F32),
        compiler_params=_cparams("parallel"),
        name="cmp_finish",
    )(x, pe8, w1, w2)


def _flash_kernel(*refs, mode, tq, tk, n_k, scale, causal, window, bias, sel, lam_init):
    it = iter(refs)
    q_ref, k_ref, vt_ref = next(it), next(it), next(it)
    cq_ref = ck_ref = sel_ref = e_ref = lam_ref = g_ref = None
    if bias:
        cq_ref, ck_ref = next(it), next(it)
    if sel:
        sel_ref, e_ref = next(it), next(it)
    if mode == "diff":
        lam_ref, g_ref = next(it), next(it)
    o_ref, m_ref, l_ref, acc_ref = next(it), next(it), next(it), next(it)

    n_sub = 1 if mode == "full" else 2
    qi = pl.program_id(2)
    lane = lax.broadcasted_iota(jnp.int32, (1, LANES), 1)
    q = q_ref[...].astype(F32) * (scale * LOG2E)
    if n_sub == 2:
        qs = [jnp.where(lane < HEAD_DIM, q, 0.0).astype(BF), jnp.where(lane >= HEAD_DIM, q, 0.0).astype(BF)]
    else:
        qs = [q.astype(BF)]
    qpos = qi * tq + lax.broadcasted_iota(jnp.int32, (1, tq), 1)

    m_ref[...] = jnp.full(m_ref.shape, NEG_INF, F32)
    l_ref[...] = jnp.zeros(l_ref.shape, F32)
    acc_ref[...] = jnp.zeros(acc_ref.shape, F32)

    if sel:
        sels = [sel_ref[e if sel == "per_head" else 0] for e in range(n_sub)]

    def chunk(j, masked):
        off = pl.multiple_of(j * tk, tk)
        kb = k_ref[pl.ds(off, tk), :]
        vt = vt_ref[:, pl.ds(off, tk)]
        mask = None
        if masked:
            kpos = off + lax.broadcasted_iota(jnp.int32, (tk, 1), 0)
            mask = kpos <= qpos
            if window:
                mask = jnp.logical_and(mask, qpos - kpos <= WINDOW)
        ss = [lax.dot_general(kb, qs[e], (((1,), (1,)), ((), ())), preferred_element_type=F32)
              for e in range(n_sub)]
        for e in range(n_sub):
            if bias:
                ss[e] = ss[e] - ck_ref[pl.ds(off, tk), e:e + 1]
            me = mask
            if sel:
                selm = jnp.dot(e_ref[pl.ds(off, tk), :], sels[e], preferred_element_type=F32) > 0.5
                me = selm if mask is None else jnp.logical_and(selm, mask)
            if me is not None:
                ss[e] = jnp.where(me, ss[e], NEG_INF)
        m_prev = [m_ref[e] for e in range(n_sub)]
        m_new, ps = [], []
        for e in range(n_sub):
            top = jnp.max(ss[e], axis=0, keepdims=True)
            if bias:
                top = top + cq_ref[e:e + 1, :]
            m_new.append(jnp.maximum(m_prev[e], top))
            shift = m_new[e] - cq_ref[e:e + 1, :] if bias else m_new[e]
            ps.append(jnp.exp2(ss[e] - shift))
        pvs = [jnp.dot(vt, ps[e].astype(BF), preferred_element_type=F32) for e in range(n_sub)]
        for e in range(n_sub):
            alpha = jnp.exp2(m_prev[e] - m_new[e])
            l_ref[e] = alpha * l_ref[e] + jnp.sum(ps[e], axis=0, keepdims=True)
            acc_ref[e] = alpha * acc_ref[e] + pvs[e]
            m_ref[e] = m_new[e]

    def masked_body(j, carry):
        chunk(j, True)
        return carry

    def open_body(j, carry):
        chunk(j, False)
        return carry

    if not causal:
        lax.fori_loop(0, n_k, open_body, 0)
    elif window:
        lax.fori_loop(jnp.maximum(qi * tq - WINDOW, 0) // tk, ((qi + 1) * tq + tk - 1) // tk, masked_body, 0)
    else:
        n_open = (qi * tq) // tk
        lax.fori_loop(0, n_open, open_body, 0)
        lax.fori_loop(n_open, ((qi + 1) * tq + tk - 1) // tk, masked_body, 0)

    outs = []
    for e in range(n_sub):
        l = l_ref[e]
        outs.append(acc_ref[e] * jnp.where(l > 0.0, 1.0 / l, 0.0))
    if mode == "full":
        ot = outs[0]
    elif mode == "pair":
        sub = lax.broadcasted_iota(jnp.int32, (LANES, 1), 0)
        ot = jnp.where(sub < HEAD_DIM, outs[0], outs[1])
    else:
        ot = outs[0] - lam_ref[0] * outs[1]
    o = ot.T
    if mode == "diff":
        y = o * lax.rsqrt(jnp.mean(o * o, axis=-1, keepdims=True) + NORM_EPS)
        o = (y * g_ref[...]) * (1.0 - lam_init)
    o_ref[...] = o.astype(o_ref.dtype)


def flash_prompt(q, k, vt, *, n_batch, t_q, t_k, n_cols, q_col, k_col, v_col, mode, scale,
                 causal=True, window=False, bias=None, sel=None, sel_mode=None, emat=None,
                 lam=None, subln_g=None, lam_init=0.0, tq=256, tk=512):
    tq = min(tq, t_q)
    tk = min(tk, t_k)
    nq = t_q // tq
    n_k = t_k // tk
    in_specs = [
        pl.BlockSpec((tq, LANES), lambda b, c, i: (b * nq + i, q_col(c))),
        pl.BlockSpec((t_k, LANES), lambda b, c, i: (b, k_col(c))),
        pl.BlockSpec((None, LANES, t_k), lambda b, c, i: (b, v_col(c), 0)),
    ]
    args = [q, k, vt]
    if bias is not None:
        cq, ck = bias
        in_specs += [pl.BlockSpec((None, None, 2, tq), lambda b, c, i: (b, c, 0, i)),
                     pl.BlockSpec((None, None, t_k, 2), lambda b, c, i: (b, c, 0, 0))]
        args += [cq, ck]
    if sel is not None:
        if sel_mode == "per_head":
            in_specs.append(pl.BlockSpec((None, 2, LANES, tq), lambda b, c, i: (b, c, 0, i)))
        else:
            in_specs.append(pl.BlockSpec((None, 1, LANES, tq), lambda b, c, i: (b, c // 2, 0, i)))
        in_specs.append(pl.BlockSpec((t_k, LANES), lambda b, c, i: (0, 0)))
        args += [sel, emat]
    if mode == "diff":
        in_specs += [pl.BlockSpec(memory_space=pltpu.SMEM), pl.BlockSpec((1, LANES), lambda b, c, i: (0, 0))]
        args += [lam.reshape(1).astype(F32), subln_g.reshape(1, LANES).astype(F32)]
    n_sub = 1 if mode == "full" else 2
    kern = functools.partial(_flash_kernel, mode=mode, tq=tq, tk=tk, n_k=n_k, scale=scale, causal=causal,
                             window=window, bias=bias is not None, sel=sel_mode if sel is not None else None,
                             lam_init=lam_init)
    return pl.pallas_call(
        kern,
        grid=(n_batch, n_cols, nq),
        in_specs=in_specs,
        out_specs=pl.BlockSpec((tq, LANES), lambda b, c, i: (b * nq + i, c)),
        out_shape=jax.ShapeDtypeStruct((n_batch * t_q, n_cols * LANES), F32),
        scratch_shapes=[pltpu.VMEM((n_sub, 1, tq), F32), pltpu.VMEM((n_sub, 1, tq), F32),
                        pltpu.VMEM((n_sub, LANES, tq), F32)],
        compiler_params=_cparams("parallel", "parallel", "arbitrary"),
        name="flash_" + mode,
    )(*args)


def _cmp_attn_kernel(q_ref, k_ref, v_ref, ov_ref, o_ref, imp_ref, *, tq, scale):
    qi = pl.program_id(2)
    lane = lax.broadcasted_iota(jnp.int32, (1, LANES), 1)
    n_c = k_ref.shape[0]
    qpos = qi * tq + lax.broadcasted_iota(jnp.int32, (tq, 1), 0)
    ends = lax.broadcasted_iota(jnp.int32, (1, n_c), 1) * CMP_STRIDE + (CMP_LEN - 1)
    mask = ends <= qpos
    kb = k_ref[...].astype(BF)
    vb = v_ref[...].astype(BF)
    ov = ov_ref[...]
    imp = jnp.zeros((tq, LANES), F32)
    halves = []
    for pair in range(2):
        q = q_ref[:, pair * LANES:(pair + 1) * LANES].astype(F32) * scale
        outs = []
        for e in range(2):
            qe = (jnp.where(lane < HEAD_DIM, q, 0.0) if e == 0 else jnp.where(lane >= HEAD_DIM, q, 0.0)).astype(BF)
            s = lax.dot_general(qe, kb, (((1,), (1,)), ((), ())), preferred_element_type=F32)
            s = jnp.where(mask, s, NEG_INF)
            m = jnp.max(s, axis=-1, keepdims=True)
            p = jnp.where(mask, jnp.exp(s - m), 0.0)
            l = jnp.sum(p, axis=-1, keepdims=True)
            p = p * jnp.where(l > 0.0, 1.0 / l, 0.0)
            pb = p.astype(BF)
            outs.append(jnp.dot(pb, vb, preferred_element_type=F32))
            imp = imp + jnp.dot(pb, ov, preferred_element_type=F32)
        halves.append(jnp.where(lane < HEAD_DIM, outs[0], outs[1]))
    o_ref[:, 0:LANES] = halves[0]
    o_ref[:, LANES:2 * LANES] = halves[1]
    imp_ref[...] = imp


def cmp_attn_prompt(q, k_dup, v_dup, ov, *, n_batch, t_q, tq=256):
    tq = min(tq, t_q)
    nq = t_q // tq
    n_c = k_dup.shape[2]
    kern = functools.partial(_cmp_attn_kernel, tq=tq, scale=HEAD_DIM ** -0.5)
    return pl.pallas_call(
        kern,
        grid=(n_batch, G_NSA, nq),
        in_specs=[pl.BlockSpec((tq, 2 * LANES), lambda b, g, i: (b * nq + i, g)),
                  pl.BlockSpec((None, None, n_c, LANES), lambda b, g, i: (b, g, 0, 0)),
                  pl.BlockSpec((None, None, n_c, LANES), lambda b, g, i: (b, g, 0, 0)),
                  pl.BlockSpec((n_c, LANES), lambda b, g, i: (0, 0))],
        out_specs=[pl.BlockSpec((tq, 2 * LANES), lambda b, g, i: (b * nq + i, g)),
                   pl.BlockSpec((None, None, tq, LANES), lambda b, g, i: (b, g, i, 0))],
        out_shape=[jax.ShapeDtypeStruct((n_batch * t_q, G_NSA * 2 * LANES), F32),
                   jax.ShapeDtypeStruct((n_batch, G_NSA, t_q, LANES), F32)],
        compiler_params=_cparams("parallel", "parallel", "arbitrary"),
        name="cmp_attn",
    )(q, k_dup, v_dup, ov)


def _decode_kernel(*refs, has_bias, has_new, emit_p):
    it = iter(refs)
    q_ref, k_ref, v_ref = next(it), next(it), next(it)
    b_ref = next(it) if has_bias else None
    if has_new:
        kn_ref, vn_ref, bn_ref = next(it), next(it), next(it)
    o_ref = next(it)
    p_ref = next(it) if emit_p else None
    m_ref, l_ref, acc_ref = next(it), next(it), next(it)
    j = pl.program_id(1)
    last = pl.num_programs(1) - 1

    @pl.when(j == 0)
    def _():
        m_ref[...] = jnp.full(m_ref.shape, NEG_INF, F32)
        l_ref[...] = jnp.zeros(l_ref.shape, F32)
        acc_ref[...] = jnp.zeros(acc_ref.shape, F32)

    q = q_ref[...]

    def update(kb, vb, b):
        s = lax.dot_general(q, kb.astype(BF), (((1,), (1,)), ((), ())), preferred_element_type=F32)
        if b is not None:
            s = s + b
        m_prev = m_ref[...]
        m_new = jnp.maximum(m_prev, jnp.max(s, axis=-1, keepdims=True))
        alpha = jnp.exp(m_prev - m_new)
        p = jnp.exp(s - m_new)
        l_ref[...] = alpha * l_ref[...] + jnp.sum(p, axis=-1, keepdims=True)
        acc_ref[...] = alpha * acc_ref[...] + jnp.dot(p.astype(BF), vb.astype(BF), preferred_element_type=F32)
        m_ref[...] = m_new
        return p

    p = update(k_ref[...], v_ref[...], b_ref[...] if has_bias else None)

    @pl.when(j == last)
    def _():
        if has_new:
            update(kn_ref[...], vn_ref[...], bn_ref[...])
        inv = 1.0 / l_ref[...]
        o_ref[...] = acc_ref[...] * inv
        if emit_p:
            p_ref[...] = p * inv


def decode_attn(qbd, k, v, *, k_col, v_col, ck, cv, bias=None, new=None, emit_p=False, rk=1024):
    n_b, r = k.shape[0], k.shape[1]
    rk = min(rk, r)
    n_j = r // rk
    assert not emit_p or (n_j == 1 and new is None)
    in_specs = [pl.BlockSpec((None, LANES, ck), lambda b, j: (b, 0, 0)),
                pl.BlockSpec((None, rk, ck), lambda b, j: (b, j, k_col)),
                pl.BlockSpec((None, rk, cv), lambda b, j: (b, j, v_col))]
    args = [qbd, k, v]
    if bias is not None:
        in_specs.append(pl.BlockSpec((None, LANES, rk), lambda b, j: (b, 0, j)))
        args.append(bias)
    if new is not None:
        kn, vn, bn = new
        in_specs += [pl.BlockSpec((None, NEW_PAD, ck), lambda b, j: (b, 0, k_col)),
                     pl.BlockSpec((None, NEW_PAD, cv), lambda b, j: (b, 0, v_col)),
                     pl.BlockSpec((None, LANES, NEW_PAD), lambda b, j: (b, 0, 0))]
        args += [kn, vn, bn]
    out_specs = [pl.BlockSpec((None, LANES, cv), lambda b, j: (b, 0, 0))]
    out_shape = [jax.ShapeDtypeStruct((n_b, LANES, cv), F32)]
    if emit_p:
        out_specs.append(pl.BlockSpec((None, LANES, rk), lambda b, j: (b, 0, 0)))
        out_shape.append(jax.ShapeDtypeStruct((n_b, LANES, r), F32))
    kern = functools.partial(_decode_kernel, has_bias=bias is not None, has_new=new is not None, emit_p=emit_p)
    out = pl.pallas_call(
        kern,
        grid=(n_b, n_j),
        in_specs=in_specs,
        out_specs=out_specs,
        out_shape=out_shape,
        scratch_shapes=[pltpu.VMEM((LANES, 1), F32), pltpu.VMEM((LANES, 1), F32), pltpu.VMEM((LANES, cv), F32)],
        compiler_params=_cparams("parallel", "arbitrary"),
        name="decode_attn",
    )(*args)
    return out if emit_p else out[0]


def _paged_kernel(*refs, pp, n_pages, rpp, mode, blk_shift, has_new):
    it = iter(refs)
    pt_ref, q_ref = next(it), next(it)
    kv_refs = [next(it) for _ in range(pp)]
    b_refs = sel_ref = f_ref = np_ref = None
    if mode == "dense":
        b_refs = [next(it) for _ in range(pp)]
    if mode == "sel":
        sel_ref = next(it)
    if mode == "fox":
        f_ref = next(it)
        np_ref = next(it)
    if has_new:
        kn_ref, vn_ref, bn_ref = next(it), next(it), next(it)
    o_ref, m_ref, l_ref, acc_ref = next(it), next(it), next(it), next(it)
    carry_ref = next(it) if mode == "fox" else None
    step = pl.program_id(1)
    last = pl.num_programs(1) - 1

    @pl.when(step == 0)
    def _():
        m_ref[...] = jnp.full(m_ref.shape, NEG_INF, F32)
        l_ref[...] = jnp.zeros(l_ref.shape, F32)
        acc_ref[...] = jnp.zeros(acc_ref.shape, F32)
        if mode == "fox":
            carry_ref[...] = jnp.zeros(carry_ref.shape, F32)

    q = q_ref[...]
    nr = q.shape[0]
    width = pp * rpp
    kcat = jnp.concatenate([kv_refs[i][0] for i in range(pp)], axis=1).astype(BF)
    s = jnp.dot(q, kcat, preferred_element_type=F32)
    if mode == "dense":
        s = s + jnp.concatenate([b_refs[i][...] for i in range(pp)], axis=1)
    if mode == "sel":
        col = lax.broadcasted_iota(jnp.int32, (1, width), 1)
        rshift = int(math.log2(rpp))
        key = (n_pages - 1 - step * pp - jnp.right_shift(col, rshift)) * rpp + jnp.bitwise_and(col, rpp - 1)
        expand = (lax.broadcasted_iota(jnp.int32, (LANES, width), 0) == jnp.right_shift(key, blk_shift)).astype(BF)
        s = s + jnp.dot(sel_ref[...], expand, preferred_element_type=F32)
    if mode == "fox":
        n_h = f_ref.shape[1]
        lf = [f_ref[pp - 1 - i] for i in range(pp)]
        upper = (lax.broadcasted_iota(jnp.int32, (rpp, rpp), 0) >
                 lax.broadcasted_iota(jnp.int32, (rpp, rpp), 1)).astype(BF)
        lf_all = jnp.concatenate(lf + lf if pp * n_h < 16 else lf, axis=0)
        hi = lf_all.astype(BF)
        lo = (lf_all - hi.astype(F32)).astype(BF)
        suffix = jnp.dot(hi, upper, preferred_element_type=F32) + jnp.dot(lo, upper, preferred_element_type=F32)
        carry = carry_ref[...]
        parts = []
        for i in range(pp):
            parts.append(suffix[i * n_h:(i + 1) * n_h] + carry)
            carry = carry + jnp.sum(lf[i], axis=-1, keepdims=True)
        carry_ref[...] = carry
        b8 = jnp.concatenate(parts, axis=1)
        s = s + (jnp.concatenate([b8] * (nr // n_h), axis=0) + jnp.concatenate([np_ref[...]] * pp, axis=1))

    def accumulate(s, pv_of):
        m_prev = m_ref[...]
        m_new = jnp.maximum(m_prev, jnp.max(s, axis=-1, keepdims=True))
        alpha = jnp.exp(m_prev - m_new)
        p = jnp.exp(s - m_new)
        l_ref[...] = alpha * l_ref[...] + jnp.sum(p, axis=-1, keepdims=True)
        acc_ref[...] = alpha * acc_ref[...] + pv_of(p.astype(BF))
        m_ref[...] = m_new

    vcat = jnp.concatenate([kv_refs[i][1] for i in range(pp)], axis=1).astype(BF)
    accumulate(s, lambda p: lax.dot_general(p, vcat, (((1,), (1,)), ((), ())), preferred_element_type=F32))

    @pl.when(step == last)
    def _():
        if has_new:
            sn = lax.dot_general(q, kn_ref[...].astype(BF), (((1,), (1,)), ((), ())),
                                 preferred_element_type=F32) + bn_ref[...]
            accumulate(sn, lambda p: jnp.dot(p, vn_ref[...].astype(BF), preferred_element_type=F32))
        o_ref[...] = acc_ref[...] * (1.0 / l_ref[...])


def paged_decode(pt, qbd, cache_t, *, layer, kv_pair, pp=8, mode=None, bias=None, selbias=None,
                 blk=None, logf=None, newpre=None, new=None):
    n_b, n_pages = pt.shape
    nr = qbd.shape[1]
    c, rpp = cache_t.shape[3], cache_t.shape[4]
    pp = min(pp, n_pages)
    assert n_pages % pp == 0
    n_steps = n_pages // pp

    def page_map(b, s, pt_ref, *, i):
        return (layer, pt_ref[b, n_pages - 1 - (s * pp + i)], kv_pair, 0, 0)

    in_specs = [pl.BlockSpec((None, nr, c), lambda b, s, pt_ref: (b, 0, 0))]
    args = [qbd]
    for i in range(pp):
        in_specs.append(pl.BlockSpec((None, None, 2, c, rpp), functools.partial(page_map, i=i)))
        args.append(cache_t)
    if mode == "dense":
        shared = bias.shape[0] == 1
        for i in range(pp):
            in_specs.append(pl.BlockSpec(
                (None, nr, rpp),
                functools.partial(lambda b, s, pt_ref, *, i: (0 if shared else b, 0, n_pages - 1 - (s * pp + i)), i=i)))
            args.append(bias)
    if mode == "sel":
        in_specs.append(pl.BlockSpec((None, nr, LANES), lambda b, s, pt_ref: (b, 0, 0)))
        args.append(selbias)
    if mode == "fox":
        n_h = logf.shape[2]
        in_specs.append(pl.BlockSpec((None, pp, n_h, rpp), lambda b, s, pt_ref: (b, n_steps - 1 - s, 0, 0)))
        args.append(logf)
        in_specs.append(pl.BlockSpec((None, nr, LANES), lambda b, s, pt_ref: (b, 0, 0)))
        args.append(newpre)
    if new is not None:
        kn, vn, bn = new
        in_specs += [pl.BlockSpec((None, NEW_PAD, c), lambda b, s, pt_ref: (b, 0, 0)),
                     pl.BlockSpec((None, NEW_PAD, c), lambda b, s, pt_ref: (b, 0, 0)),
                     pl.BlockSpec((None, nr, NEW_PAD), lambda b, s, pt_ref: (b, 0, 0))]
        args += [kn, vn, bn]
    scratch = [pltpu.VMEM((nr, 1), F32), pltpu.VMEM((nr, 1), F32), pltpu.VMEM((nr, c), F32)]
    if mode == "fox":
        scratch.append(pltpu.VMEM((logf.shape[2], 1), F32))
    grid_spec = pltpu.PrefetchScalarGridSpec(
        num_scalar_prefetch=1,
        grid=(n_b, n_steps),
        in_specs=in_specs,
        out_specs=pl.BlockSpec((None, nr, c), lambda b, s, pt_ref: (b, 0, 0)),
        scratch_shapes=scratch,
    )
    kern = functools.partial(_paged_kernel, pp=pp, n_pages=n_pages, rpp=rpp, mode=mode,
                             blk_shift=None if blk is None else int(math.log2(blk)), has_new=new is not None)
    return pl.pallas_call(
        kern,
        grid_spec=grid_spec,
        out_shape=jax.ShapeDtypeStruct((n_b, nr, c), F32),
        compiler_params=_cparams("parallel", "arbitrary"),
        name="paged_" + (mode or "plain"),
    )(pt, *args)


def _rows_kernel(*refs, pp, v_shift, has_new):
    it = iter(refs)
    pt_ref, q_ref = next(it), next(it)
    x_refs = [next(it) for _ in range(pp)]
    b_ref = next(it)
    if has_new:
        xn_ref, bn_ref = next(it), next(it)
    o_ref, m_ref, l_ref, acc_ref = next(it), next(it), next(it), next(it)
    step = pl.program_id(1)
    last = pl.num_programs(1) - 1

    @pl.when(step == 0)
    def _():
        m_ref[...] = jnp.full(m_ref.shape, NEG_INF, F32)
        l_ref[...] = jnp.zeros(l_ref.shape, F32)
        acc_ref[...] = jnp.zeros(acc_ref.shape, F32)

    q = q_ref[...]

    def accumulate(x, bias):
        s = lax.dot_general(q, x, (((1,), (1,)), ((), ())), preferred_element_type=F32) + bias
        m_prev = m_ref[...]
        m_new = jnp.maximum(m_prev, jnp.max(s, axis=-1, keepdims=True))
        alpha = jnp.exp(m_prev - m_new)
        p = jnp.exp(s - m_new)
        l_ref[...] = alpha * l_ref[...] + jnp.sum(p, axis=-1, keepdims=True)
        pv = jnp.dot(pltpu.roll(p, v_shift, 1).astype(BF), x, preferred_element_type=F32)
        acc_ref[...] = alpha * acc_ref[...] + pv
        m_ref[...] = m_new

    accumulate(jnp.concatenate([x_refs[i][...] for i in range(pp)], axis=0).astype(BF), b_ref[...])

    @pl.when(step == last)
    def _():
        if has_new:
            accumulate(xn_ref[...].astype(BF), bn_ref[...])
        o_ref[...] = acc_ref[...] * (1.0 / l_ref[...])


def paged_rows(pt, q, cache_rows, bias, *, layer, v_shift, pp=8, new=None):
    n_b, n_pages = pt.shape
    nr = q.shape[1]
    rows = cache_rows.shape[2]
    pp = min(pp, n_pages)
    assert n_pages % pp == 0
    in_specs = [pl.BlockSpec((None, nr, LANES), lambda b, s, pt_ref: (b, 0, 0))]
    args = [q]
    for i in range(pp):
        in_specs.append(pl.BlockSpec(
            (None, None, rows, LANES),
            functools.partial(lambda b, s, pt_ref, *, i: (layer, pt_ref[b, s * pp + i], 0, 0), i=i)))
        args.append(cache_rows)
    in_specs.append(pl.BlockSpec((nr, pp * rows), lambda b, s, pt_ref: (0, 0)))
    args.append(jnp.concatenate([bias] * pp, axis=1))
    if new is not None:
        in_specs += [pl.BlockSpec((None, NEW_PAD, LANES), lambda b, s, pt_ref: (b, 0, 0)),
                     pl.BlockSpec((nr, NEW_PAD), lambda b, s, pt_ref: (0, 0))]
        args += list(new)
    grid_spec = pltpu.PrefetchScalarGridSpec(
        num_scalar_prefetch=1,
        grid=(n_b, n_pages // pp),
        in_specs=in_specs,
        out_specs=pl.BlockSpec((None, nr, LANES), lambda b, s, pt_ref: (b, 0, 0)),
        scratch_shapes=[pltpu.VMEM((nr, 1), F32), pltpu.VMEM((nr, 1), F32), pltpu.VMEM((nr, LANES), F32)],
    )
    return pl.pallas_call(
        functools.partial(_rows_kernel, pp=pp, v_shift=v_shift, has_new=new is not None),
        grid_spec=grid_spec,
        out_shape=jax.ShapeDtypeStruct((n_b, nr, LANES), F32),
        compiler_params=_cparams("parallel", "arbitrary"),
        name="paged_rows",
    )(pt, *args)


def _block_sum_kernel(*refs, pp, ppb):
    pt_ref = refs[0]
    k_refs = refs[1:1 + pp]
    o_ref = refs[1 + pp]
    ones = jnp.ones((8, k_refs[0].shape[1]), BF)
    rows = []
    for bi in range(pp // ppb):
        acc = None
        for j in range(ppb):
            x = k_refs[bi * ppb + j][...]
            hi = x.astype(BF)
            lo = (x - hi.astype(F32)).astype(BF)
            t = (lax.dot_general(ones, hi, (((1,), (1,)), ((), ())), preferred_element_type=F32) +
                 lax.dot_general(ones, lo, (((1,), (1,)), ((), ())), preferred_element_type=F32))
            acc = t if acc is None else acc + t
        rows.append(acc[0:1])
    o_ref[...] = jnp.concatenate(rows, axis=0)


def page_block_sums(pt, cache_t, *, layer, slot, ppb, pp=16):
    n_b, n_pages = pt.shape
    c, rpp = cache_t.shape[3], cache_t.shape[4]
    pp = min(pp, n_pages)
    assert n_pages % pp == 0 and pp % ppb == 0
    in_specs = []
    for i in range(pp):
        in_specs.append(pl.BlockSpec(
            (None, None, None, c, rpp),
            functools.partial(lambda b, s, pt_ref, *, i: (layer, pt_ref[b, s * pp + i], slot, 0, 0), i=i)))
    grid_spec = pltpu.PrefetchScalarGridSpec(
        num_scalar_prefetch=1,
        grid=(n_b, n_pages // pp),
        in_specs=in_specs,
        out_specs=pl.BlockSpec((None, pp // ppb, c), lambda b, s, pt_ref: (b, s, 0)),
    )
    return pl.pallas_call(
        functools.partial(_block_sum_kernel, pp=pp, ppb=ppb),
        grid_spec=grid_spec,
        out_shape=jax.ShapeDtypeStruct((n_b, n_pages // ppb, c), F32),
        compiler_params=_cparams("parallel", "arbitrary"),
        name="page_block_sums",
    )(pt, *([cache_t] * pp))


def _merge_kernel(a_ref, b_ref, c_ref, d_ref, w_ref, g0_ref, g1_ref, g2_ref, g3_ref, o_ref):
    acc = None
    for n, (br, g) in enumerate(((a_ref, g0_ref), (b_ref, g1_ref), (c_ref, g2_ref), (d_ref, g3_ref))):
        y = jnp.dot(br[...].astype(BF), w_ref[n].astype(BF), preferred_element_type=F32)
        t = jax.nn.sigmoid(g[...].astype(F32)) * y
        acc = t if acc is None else acc + t
    o_ref[...] = acc.astype(o_ref.dtype)


def merge_branches(branches, w_branch, glog, d_model, layer):
    n, bw = branches[0].shape
    tm = _tile(n, 640)
    tn = _tile(d_model, 512, LANES)
    nj = d_model // tn
    in_specs = [pl.BlockSpec((tm, bw), lambda i, j: (i, 0)) for _ in range(N_BRANCH)]
    in_specs.append(pl.BlockSpec((None, N_BRANCH, bw, tn), lambda i, j: (layer, 0, 0, j)))
    for nb in range(N_BRANCH):
        in_specs.append(pl.BlockSpec((tm, tn), functools.partial(lambda i, j, nb: (i, nb * nj + j), nb=nb)))
    return pl.pallas_call(
        _merge_kernel,
        grid=(n // tm, nj),
        in_specs=in_specs,
        out_specs=pl.BlockSpec((tm, tn), lambda i, j: (i, j)),
        out_shape=jax.ShapeDtypeStruct((n, d_model), BF),
        compiler_params=_cparams("parallel", "arbitrary"),
        name="merge",
    )(*branches, w_branch, glog, glog, glog, glog)


def _moe_kernel(be_ref, x_ref, wg_ref, wu_ref, wd_ref, o_ref):
    x = x_ref[...].astype(BF)
    g = jnp.dot(x, wg_ref[...].astype(BF), preferred_element_type=F32)
    u = jnp.dot(x, wu_ref[...].astype(BF), preferred_element_type=F32)
    hid = (jax.nn.silu(g) * u).astype(BF)
    o_ref[...] = jnp.dot(hid, wd_ref[...].astype(BF), preferred_element_type=F32)


def moe_experts(xs, blk_e, w_g, w_u, w_d, layer):
    rows, d = xs.shape
    f = w_g.shape[3]
    n_blk = rows // MOE_BLOCK
    grid_spec = pltpu.PrefetchScalarGridSpec(
        num_scalar_prefetch=1,
        grid=(n_blk,),
        in_specs=[pl.BlockSpec((MOE_BLOCK, d), lambda i, be: (i, 0)),
                  pl.BlockSpec((None, None, d, f), lambda i, be: (layer, be[i], 0, 0)),
                  pl.BlockSpec((None, None, d, f), lambda i, be: (layer, be[i], 0, 0)),
                  pl.BlockSpec((None, None, f, d), lambda i, be: (layer, be[i], 0, 0))],
        out_specs=pl.BlockSpec((MOE_BLOCK, d), lambda i, be: (i, 0)),
    )
    return pl.pallas_call(
        _moe_kernel,
        grid_spec=grid_spec,
        out_shape=jax.ShapeDtypeStruct((rows, d), F32),
        compiler_params=_cparams("arbitrary"),
        name="moe_experts",
    )(blk_e, xs, w_g, w_u, w_d)


def _rope_tables(pos):
    half = ROPE_DIMS // 2
    inv_freq = jnp.exp(jnp.arange(half, dtype=F32) * (-2.0 * math.log(ROPE_THETA) / ROPE_DIMS))
    ang = jnp.asarray(pos, F32)[:, None] * inv_freq[None, :]
    return jnp.cos(ang), jnp.sin(ang)


def _rope(x, cos, sin):
    half = ROPE_DIMS // 2
    c = cos[:, None, :]
    s = sin[:, None, :]
    x1, x2 = x[..., :half], x[..., half:ROPE_DIMS]
    return jnp.concatenate([x1 * c - x2 * s, x2 * c + x1 * s, x[..., ROPE_DIMS:]], axis=-1)


def _overlap_matrix(n_cmp, n_slc):
    st = np.arange(n_cmp) * CMP_STRIDE
    en = st + CMP_LEN
    j = np.arange(n_slc) * SEL_BLOCK
    return ((st[:, None] < j[None, :] + SEL_BLOCK) & (en[:, None] > j[None, :])).astype(np.float32)


def _block_expand_t(n_keys, block):
    return ((np.arange(n_keys) // block)[:, None] == np.arange(LANES)[None, :]).astype(np.float32)


def _top_mask(score, k):
    n = score.shape[-1]
    idx = jnp.arange(n, dtype=jnp.int32)
    mine = score[..., :, None]
    other = score[..., None, :]
    ahead = (other > mine) | ((other == mine) & (idx[None, :] < idx[:, None]))
    return jnp.sum(ahead.astype(jnp.int32), axis=-1) < k


def _nsa_select(imp, q_pos, n_slc):
    own = q_pos // SEL_BLOCK
    j = jnp.arange(n_slc, dtype=jnp.int32)[None, :]
    valid = j <= own[:, None]
    forced = (j == 0) | (j == own[:, None]) | (j == own[:, None] - 1)
    score = jnp.where(valid, imp + jnp.where(forced, SEL_BONUS, 0.0), -SEL_BONUS)
    return _top_mask(score, min(N_SEL, n_slc)) & valid[None, None]


def _moba_select(sc, q_pos, n_blk):
    own = q_pos // MOBA_BLOCK
    past_blk = jnp.arange(n_blk, dtype=jnp.int32)[None, :] < own[:, None]
    sc = jnp.where(past_blk[None, None], sc, NEG_INF)
    return _top_mask(sc, min(MOBA_TOPK, n_blk)) & past_blk[None, None]


def _pad_lanes(x, n=LANES):
    return jnp.pad(x, [(0, 0)] * (x.ndim - 1) + [(0, n - x.shape[-1])])


def _pad_rows(x, n=LANES):
    return jnp.pad(x, [(0, 0)] * (x.ndim - 2) + [(0, n - x.shape[-2]), (0, 0)])


def _compress(k, pe, w1, w2):
    b, tk, g, d = k.shape
    n_cmp = (tk - CMP_LEN) // CMP_STRIDE + 1
    idx = np.arange(n_cmp)[:, None] * CMP_STRIDE + np.arange(CMP_LEN)[None, :]
    blk = k[:, idx] + pe[None, None, :, None, :]
    blk = jnp.transpose(blk, (0, 1, 3, 2, 4)).reshape(b * n_cmp * g, CMP_LEN * d)
    return cmp_mlp(blk, w1, w2).reshape(b, n_cmp, g, d)


def _key_minor(cache):
    l, p, r, a, b, d = cache.shape
    return jnp.transpose(cache, (0, 1, 3, 4, 5, 2)).reshape(l, p, a, b * d, r)


def _qbd(q, n_heads, d_head, scale):
    b, tq, c = q.shape
    qh = q.reshape(b, tq, n_heads, d_head) * scale
    eye = jnp.eye(n_heads, dtype=q.dtype)
    bd = jnp.einsum("bthd,hg->bthgd", qh, eye).reshape(b, tq * n_heads, c)
    return _pad_rows(bd, DEC_ROWS).astype(BF)


def _diag_blocks(o, n_heads, tq, d_out):
    b = o.shape[0]
    o = o[:, :tq * n_heads].reshape(b, tq, n_heads, n_heads, d_out)
    idx = jnp.arange(n_heads)
    return o[:, :, idx, idx]


def _group_blocks(o, tq, hpg):
    b = o.shape[0]
    o = o[:, :tq * G_NSA * hpg].reshape(b, tq, G_NSA, hpg, G_NSA, HEAD_DIM)
    gi = jnp.arange(G_NSA)
    o = o[:, :, gi, :, gi]
    return jnp.transpose(o, (1, 2, 0, 3, 4)).reshape(b, tq, G_NSA * hpg * HEAD_DIM)


def _query_rows(x):
    b, t, h, r = x.shape
    return _pad_rows(x.reshape(b, t * h, r), DEC_ROWS)


def kernel(x_prompt, x_sample, cache_diff_kv, cache_fox_kv, cache_fox_logf, cache_nsa_kv, cache_nsa_win,
           cache_moba_kv, cache_mem_kv, page_table, mem_prompt, norm_mix_g, w_in, b_forget, diff_lambda,
           diff_subln_g, nsa_cmp_pe, nsa_cmp_w1, nsa_cmp_w2, w_branch, w_out, norm_mem_g, norm_memkv_g,
           w_mem_q, w_mem_kv, w_mem_o, norm_ffn_g, w_router_group, b_router_group, w_router_expert,
           b_router_expert, w_exp_gate, w_exp_up, w_exp_down, norm_final_g):
    depth = w_in.shape[0]
    bp, t, d_model = x_prompt.shape
    bs, ts, _ = x_sample.shape
    n_pool, page = cache_diff_kv.shape[1], cache_diff_kv.shape[2]
    n_pages = page_table.shape[1]
    past_len = n_pages * page
    bw = d_model // N_BRANCH
    h_diff = bw // (2 * HEAD_DIM)
    h8 = bw // HEAD_DIM
    hpg = h8 // G_NSA
    gw = G_NSA * HEAD_DIM
    n_mem = mem_prompt.shape[1]
    mem_hd = bw // H_MEM
    n_exp = N_GROUPS * EXPERTS_PER_GROUP
    np_tok = bp * t
    ns_tok = bs * ts
    n_tok = np_tok + ns_tok
    assert h8 * ts <= DEC_ROWS and ts < CMP_STRIDE and past_len % MOBA_BLOCK == 0 and page % CMP_STRIDE == 0
    assert past_len // SEL_BLOCK <= LANES and DEC_ROWS % h8 == 0 and 2 * h_diff == h8 and page == LANES
    scale = HEAD_DIM ** -0.5
    page_table = page_table.astype(jnp.int32)

    pos_p = np.arange(t, dtype=np.int32)
    pos_s = past_len + np.arange(ts, dtype=np.int32)
    pos_all = np.concatenate([np.tile(pos_p, bp), np.tile(pos_s, bs)])
    cos, sin = _rope_tables(pos_all)

    sizes = (bw, bw, bw, bw, bw, bw, h8, bw, 6 * G_NSA * HEAD_DIM, 3 * h8, bw, bw, bw, N_BRANCH * d_model)
    offs = np.concatenate([[0], np.cumsum(sizes)])
    seg = lambda i: (int(offs[i]), int(offs[i + 1]))
    main_ids = (0, 1, 2, 3, 4, 5, 7, 8, 10, 11, 12)
    small_ids = (6, 9)
    n_main = sum(sizes[i] for i in main_ids)
    n_small = sum(sizes[i] for i in small_ids)
    n_main_pad = -(-(n_main + n_small) // 512) * 512
    mo = {}
    o = 0
    for i in main_ids + small_ids:
        mo[i] = o
        o += sizes[i]

    x = jnp.concatenate([x_prompt.reshape(np_tok, d_model), x_sample.reshape(ns_tok, d_model)], axis=0)

    e_moba = jnp.asarray(_block_expand_t(t, MOBA_BLOCK), BF)
    e_nsa = jnp.asarray(_block_expand_t(t, SEL_BLOCK), BF)
    n_cmp_p = (t - CMP_LEN) // CMP_STRIDE + 1
    n_cmp_p_pad = -(-n_cmp_p // LANES) * LANES
    n_slc_p = -(-t // SEL_BLOCK)
    ov_p = np.zeros((n_cmp_p_pad, LANES), np.float32)
    ov_p[:n_cmp_p, :n_slc_p] = _overlap_matrix(n_cmp_p, n_slc_p)
    ov_p = jnp.asarray(ov_p, BF)
    n_mblk_p = -(-t // MOBA_BLOCK)

    tk_s = past_len + ts
    n_cmp_s = (tk_s - CMP_LEN) // CMP_STRIDE + 1
    n_cmp_s_pad = -(-n_cmp_s // LANES) * LANES
    n_slc_s = -(-tk_s // SEL_BLOCK)
    n_slc_past = past_len // SEL_BLOCK
    ov_s = jnp.asarray(_overlap_matrix(n_cmp_s, n_slc_s))
    n_mblk_s = -(-tk_s // MOBA_BLOCK)
    n_mblk_past = past_len // MOBA_BLOCK
    def row_query(n_rows, n_heads):
        return np.minimum(np.arange(n_rows) // n_heads, ts - 1)

    row_t = row_query(DEC_ROWS, h8)
    used = np.arange(DEC_ROWS) < h8 * ts

    def new_bias(n_heads):
        keyi = np.arange(NEW_PAD)
        okm = (keyi[None, :] <= row_query(DEC_ROWS, n_heads)[:, None]) & (keyi[None, :] < ts)
        return np.where(okm, 0.0, NEG_INF).astype(np.float32)

    def new_rows(a):
        return _pad_rows(a.reshape(bs, ts, a.shape[-1]), NEW_PAD)

    fox_t = _key_minor(cache_fox_kv)
    moba_t = _key_minor(cache_moba_kv)
    nsa_t = _key_minor(cache_nsa_kv)
    win_t = _key_minor(cache_nsa_win)
    logf_t = jnp.transpose(cache_fox_logf, (0, 1, 3, 2))
    diff_rows = cache_diff_kv.reshape(depth, n_pool, page * 2 * h_diff, 2 * HEAD_DIM)
    mem_rows = cache_mem_kv.reshape(depth, bs, n_mem * 2 * H_MEM, mem_hd)
    slot_d = np.arange(page * 2 * h_diff) % (2 * h_diff)
    head_d = (np.arange(DEC_ROWS) % h8) // 2
    diff_bias = jnp.asarray(np.where(slot_d[None, :] == head_d[:, None], 0.0, NEG_INF).astype(np.float32))
    slot_n = np.arange(NEW_PAD) % (2 * h_diff)
    tok_n = np.arange(NEW_PAD) // (2 * h_diff)
    diff_ok_n = ((slot_n[None, :] == head_d[:, None]) & (tok_n[None, :] <= row_query(DEC_ROWS, h8)[:, None])
                 & (tok_n[None, :] < ts))
    diff_bias_new = jnp.asarray(np.where(diff_ok_n, 0.0, NEG_INF).astype(np.float32))
    slot_m = np.arange(n_mem * 2 * H_MEM) % (2 * H_MEM)
    mem_bias = jnp.asarray(np.where(slot_m[None, :] == (np.arange(DEC_ROWS) % H_MEM)[:, None], 0.0,
                                    NEG_INF).astype(np.float32))
    comp_eye = jnp.eye(2, dtype=F32)
    wb = cache_nsa_win.shape[2]
    own_pt = jnp.arange(bs, dtype=jnp.int32).reshape(bs, 1)
    cpp = page // CMP_STRIDE
    nsa_chunks = cache_nsa_kv[:, :, :, 0:2].reshape(depth, n_pool, cpp, CMP_STRIDE, 2, G_NSA, HEAD_DIM)
    nsa_chunks = jnp.transpose(nsa_chunks, (0, 4, 1, 2, 5, 3, 6)).reshape(
        depth, 2, n_pool * cpp * G_NSA, CMP_STRIDE * HEAD_DIM)

    win_ok = (np.arange(wb)[None, :] >= row_t[:, None] + (wb - WINDOW)) | ~used[:, None]
    win_bias = jnp.asarray(np.where(win_ok, 0.0, NEG_INF).astype(np.float32)).reshape(1, DEC_ROWS, wb)
    ends_s = np.arange(n_cmp_s_pad) * CMP_STRIDE + (CMP_LEN - 1)
    cmp_ok = ((ends_s[None, :] <= (past_len + row_query(LANES, h8))[:, None])
              & (np.arange(n_cmp_s_pad)[None, :] < n_cmp_s))
    cmp_bias = jnp.broadcast_to(jnp.asarray(np.where(cmp_ok, 0.0, NEG_INF).astype(np.float32)),
                                (bs, LANES, n_cmp_s_pad))

    acc_p = [[] for _ in range(7)]
    acc_s = [[] for _ in range(6)]

    for l in range(depth):
        lam_init = 0.8 - 0.6 * math.exp(-0.3 * l)
        lam_p = diff_lambda[l].astype(F32)
        lam = jnp.exp(jnp.sum(lam_p[0] * lam_p[1])) - jnp.exp(jnp.sum(lam_p[2] * lam_p[3])) + lam_init

        wl = w_in[l]
        w_main = jnp.concatenate([wl[:, seg(i)[0]:seg(i)[1]] for i in main_ids + small_ids], axis=1)
        w_main = jnp.pad(w_main, ((0, 0), (0, n_main_pad - n_main - n_small))).astype(BF)
        w_gate = wl[:, seg(13)[0]:seg(13)[1]].astype(BF)
        h = rmsnorm(x, norm_mix_g[l], BF)
        proj = matmul(h, w_main)
        glog = matmul(h, w_gate, tn_target=1024)

        def col(i):
            return proj[:, mo[i]:mo[i] + sizes[i]]

        dq = _rope(col(0).reshape(n_tok, 2 * h_diff, HEAD_DIM), cos, sin).reshape(n_tok, bw)
        dk = _rope(col(1).reshape(n_tok, 2 * h_diff, HEAD_DIM), cos, sin).reshape(n_tok, bw)
        diff_new = jnp.concatenate([dk, col(2)], axis=1)
        fq = col(3)
        fox_new = jnp.concatenate([col(4), col(5)], axis=1)
        logf_new = jax.nn.log_sigmoid(col(6).astype(F32) + b_forget[l].astype(F32))
        nq = _rope(col(7).reshape(n_tok, h8, HEAD_DIM), cos, sin).reshape(n_tok, bw)
        nkv = col(8).reshape(n_tok, 6, G_NSA, HEAD_DIM)
        nsa_new = jnp.stack([nkv[:, 0], nkv[:, 1], _rope(nkv[:, 2], cos, sin), nkv[:, 3]], axis=1)
        win_new = jnp.stack([_rope(nkv[:, 4], cos, sin), nkv[:, 5]], axis=1)
        ngate = jax.nn.sigmoid(col(9).astype(F32)).reshape(n_tok, h8, 3)
        mq = _rope(col(10).reshape(n_tok, h8, HEAD_DIM), cos, sin).reshape(n_tok, bw)
        mk = _rope(col(11).reshape(n_tok, h8, HEAD_DIM), cos, sin).reshape(n_tok, bw)
        moba_new = jnp.concatenate([mk, col(12)], axis=1)
        nsa_flat = nsa_new.reshape(n_tok, 4 * gw)
        win_flat = win_new.reshape(n_tok, 2 * gw)

        P = slice(0, np_tok)

        def keys_bf(a):
            return a.astype(BF)

        def vals_t(a):
            return jnp.transpose(a.reshape(bp, t, a.shape[-1]), (0, 2, 1)).astype(BF)

        o_a_p = flash_prompt(dq[P], keys_bf(dk[P]), vals_t(col(2)[P]), n_batch=bp, t_q=t, t_k=t, n_cols=h_diff,
                             q_col=lambda c: c, k_col=lambda c: c, v_col=lambda c: c, mode="diff",
                             scale=scale, lam=lam, subln_g=diff_subln_g[l], lam_init=lam_init)
        logf_p = logf_new[P].reshape(bp, t, h8)
        c_all = (jnp.cumsum(logf_p, axis=1) * LOG2E).reshape(bp, t, h8 // 2, 2)
        cq = jnp.transpose(c_all, (0, 2, 3, 1))
        ck = jnp.transpose(c_all, (0, 2, 1, 3))
        o_b_p = flash_prompt(fq[P], keys_bf(col(4)[P]), vals_t(col(5)[P]), n_batch=bp, t_q=t, t_k=t, n_cols=h8 // 2,
                             q_col=lambda c: c, k_col=lambda c: c, v_col=lambda c: c, mode="pair",
                             scale=scale, bias=(cq, ck))
        nsa_p = nsa_new[P].reshape(bp, t, 4, G_NSA, HEAD_DIM)
        k_cmp = _compress(nsa_p[:, :, 0], nsa_cmp_pe[l, 0], nsa_cmp_w1[l, 0], nsa_cmp_w2[l, 0])
        v_cmp = _compress(nsa_p[:, :, 1], nsa_cmp_pe[l, 1], nsa_cmp_w1[l, 1], nsa_cmp_w2[l, 1])

        def dup_cmp(a):
            a = jnp.transpose(a, (0, 2, 1, 3))
            a = jnp.concatenate([a, a], axis=-1)
            return jnp.pad(a, ((0, 0), (0, 0), (0, n_cmp_p_pad - n_cmp_p), (0, 0)))

        o_cmp_p, imp_p = cmp_attn_prompt(nq[P], dup_cmp(k_cmp), dup_cmp(v_cmp), ov_p, n_batch=bp, t_q=t)
        sel_nsa = _nsa_select(imp_p[..., :n_slc_p], jnp.asarray(pos_p), n_slc_p)
        sel_nsa = _pad_rows(jnp.transpose(sel_nsa, (0, 1, 3, 2)).astype(BF))

        def dup_g(a):
            return jnp.concatenate([a, a], axis=-1).reshape(a.shape[0], G_NSA * LANES)

        nsa_pt = nsa_new[P]
        ppg = hpg // 2
        o_sel_p = flash_prompt(nq[P], keys_bf(dup_g(nsa_pt[:, 2])), vals_t(dup_g(nsa_pt[:, 3])), n_batch=bp, t_q=t,
                               t_k=t, n_cols=h8 // 2, q_col=lambda c: c, k_col=lambda c: c // ppg,
                               v_col=lambda c: c // ppg, mode="pair", scale=scale, sel=sel_nsa,
                               sel_mode="per_group", emat=e_nsa)
        win_p = win_new[P]
        o_win_p = flash_prompt(nq[P], keys_bf(dup_g(win_p[:, 0])), vals_t(dup_g(win_p[:, 1])), n_batch=bp, t_q=t,
                               t_k=t, n_cols=h8 // 2, q_col=lambda c: c, k_col=lambda c: c // ppg,
                               v_col=lambda c: c // ppg, mode="pair", scale=scale, window=True)
        mk_p = mk[P].reshape(bp, n_mblk_p, MOBA_BLOCK, h8, HEAD_DIM)
        k_mean_p = jnp.transpose(jnp.mean(mk_p.astype(F32), axis=2), (0, 2, 1, 3))
        sc_p = jnp.einsum("bqhd,bhnd->bhqn", mq[P].reshape(bp, t, h8, HEAD_DIM).astype(F32), k_mean_p)
        sel_moba = _moba_select(sc_p, jnp.asarray(pos_p), n_mblk_p)
        own_p = jnp.asarray(pos_p // MOBA_BLOCK)
        sel_moba = sel_moba | (jnp.arange(n_mblk_p)[None, :] == own_p[:, None])[None, None]
        sel_moba = _pad_rows(jnp.transpose(sel_moba, (0, 1, 3, 2)).astype(BF))
        o_d_p = flash_prompt(mq[P], keys_bf(mk[P]), vals_t(col(12)[P]), n_batch=bp, t_q=t, t_k=t, n_cols=h8 // 2,
                             q_col=lambda c: c, k_col=lambda c: c, v_col=lambda c: c, mode="pair",
                             scale=scale, sel=sel_moba, sel_mode="per_head", emat=e_moba)

        S = slice(np_tok, n_tok)
        nb8 = jnp.broadcast_to(jnp.asarray(new_bias(h8)), (bs, DEC_ROWS, NEW_PAD))
        dq5 = dq[S].reshape(bs, ts, h_diff, 2, HEAD_DIM) * scale
        q_rows = jnp.einsum("bthcd,ce->bthced", dq5, comp_eye).reshape(bs, ts * h8, 2 * HEAD_DIM)
        q_rows = _pad_rows(q_rows, DEC_ROWS).astype(BF)
        x_new = _pad_rows(diff_new[S].reshape(bs, ts * 2 * h_diff, 2 * HEAD_DIM), NEW_PAD)
        o = paged_rows(page_table, q_rows, diff_rows, diff_bias, layer=l, v_shift=h_diff,
                       new=(x_new, diff_bias_new))
        o = o[:, :ts * h8].reshape(bs, ts, h_diff, 2, 2 * HEAD_DIM)
        o = o[:, :, :, 0] - lam * o[:, :, :, 1]
        y = o * lax.rsqrt(jnp.mean(o * o, axis=-1, keepdims=True) + NORM_EPS)
        o_a_s = ((y * diff_subln_g[l].astype(F32)) * (1.0 - lam_init)).reshape(ns_tok, bw)

        logf_s = logf_new[S].reshape(bs, ts, h8)
        pre_new = jnp.cumsum(logf_s, axis=1)
        newpre = jnp.broadcast_to(_query_rows(pre_new[..., None]), (bs, DEC_ROWS, LANES))
        d_new = pre_new[:, :, None, :] - pre_new[:, None, :, :]
        d_new = _pad_lanes(_query_rows(jnp.transpose(d_new, (0, 1, 3, 2))), NEW_PAD)
        fnew = new_rows(fox_new[S])
        qbd = _qbd(fq[S].reshape(bs, ts, bw), h8, HEAD_DIM, scale)
        o = paged_decode(page_table, qbd, fox_t, layer=l, kv_pair=0, mode="fox", logf=logf_t[l][page_table],
                         newpre=newpre, new=(fnew[..., :bw], fnew[..., bw:], d_new + nb8))
        o_b_s = _diag_blocks(o, h8, ts, HEAD_DIM).reshape(ns_tok, bw)

        kv_cmp_s = []
        for w in range(2):
            w1 = nsa_cmp_w1[l, w]
            half = CMP_STRIDE * HEAD_DIM
            w_ab = jnp.concatenate([w1[:half], w1[half:]], axis=1)
            ab = matmul(nsa_chunks[l, w], w_ab, tm_target=1024)
            hid = w1.shape[1]
            ab = ab.reshape(n_pool, cpp, G_NSA, 2 * hid)[page_table].reshape(bs, n_pages * cpp, G_NSA, 2 * hid)
            pre = ab[:, :n_cmp_s, :, :hid] + ab[:, 1:n_cmp_s + 1, :, hid:]
            kv = cmp_finish(pre.reshape(bs * n_cmp_s * G_NSA, hid), nsa_cmp_pe[l, w].reshape(-1), w1,
                            nsa_cmp_w2[l, w])
            kv_cmp_s.append(kv.reshape(bs, n_cmp_s, gw))
        kv_cmp_s = jnp.pad(jnp.concatenate(kv_cmp_s, axis=-1), ((0, 0), (0, n_cmp_s_pad - n_cmp_s), (0, 0)))
        qbd_n = _qbd(nq[S].reshape(bs, ts, bw), h8, HEAD_DIM, scale)
        qbd_g = qbd_n.reshape(bs, DEC_ROWS, G_NSA, hpg, HEAD_DIM).sum(axis=3).reshape(bs, DEC_ROWS, gw)
        o, p_c = decode_attn(_pad_rows(qbd_g, LANES), kv_cmp_s, kv_cmp_s, k_col=0, v_col=1, ck=gw, cv=gw,
                             bias=cmp_bias, emit_p=True, rk=n_cmp_s_pad)
        o_cmp_s = _group_blocks(o, ts, hpg).reshape(ns_tok, bw)
        p_c = p_c[:, :h8 * ts, :n_cmp_s].reshape(bs, ts, G_NSA, hpg, n_cmp_s)
        imp_s = jnp.einsum("bqgmn,ns->bgqs", p_c, ov_s)
        sel_s = _nsa_select(imp_s, jnp.asarray(pos_s), n_slc_s)
        sel_rows = jnp.broadcast_to(jnp.transpose(sel_s, (0, 2, 1, 3))[:, :, :, None, :n_slc_past],
                                    (bs, ts, G_NSA, hpg, n_slc_past)).reshape(bs, ts, h8, n_slc_past)
        selbias = _pad_lanes(jnp.where(_query_rows(sel_rows.astype(F32)) > 0.5, 0.0, NEG_INF))
        selbias = jnp.where(jnp.asarray(used)[None, :, None], selbias, 0.0).astype(BF)
        nnew = new_rows(nsa_flat[S])
        o = paged_decode(page_table, qbd_g, nsa_t, layer=l, kv_pair=1, mode="sel", selbias=selbias,
                         blk=SEL_BLOCK, new=(nnew[..., 2 * gw:3 * gw], nnew[..., 3 * gw:], nb8))
        o_sel_s = _group_blocks(o, ts, hpg).reshape(ns_tok, bw)
        wnew = new_rows(win_flat[S])
        o = paged_decode(own_pt, qbd_g, win_t, layer=l, kv_pair=0, mode="dense", bias=win_bias,
                         new=(wnew[..., :gw], wnew[..., gw:], nb8))
        o_win_s = _group_blocks(o, ts, hpg).reshape(ns_tok, bw)
        wbuf = cache_nsa_win[l].reshape(bs, wb, 2 * gw)
        win_state = jnp.concatenate([wbuf, win_flat[S].reshape(bs, ts, 2 * gw)], axis=1)[:, ts:]

        k_sum = page_block_sums(page_table, moba_t, layer=l, slot=0, ppb=MOBA_BLOCK // page)
        k_mean_s = (k_sum / MOBA_BLOCK).reshape(bs, n_mblk_past, h8, HEAD_DIM)
        k_mean_s = jnp.pad(jnp.transpose(k_mean_s, (0, 2, 1, 3)), ((0, 0), (0, 0), (0, n_mblk_s - n_mblk_past), (0, 0)))
        sc_s = jnp.einsum("bqhd,bhnd->bhqn", mq[S].reshape(bs, ts, h8, HEAD_DIM).astype(F32), k_mean_s)
        selm_s = _moba_select(sc_s, jnp.asarray(pos_s), n_mblk_s)
        selm_rows = _query_rows(jnp.transpose(selm_s[..., :n_mblk_past], (0, 2, 1, 3)).astype(F32))
        mbias = _pad_lanes(jnp.where(selm_rows > 0.5, 0.0, NEG_INF))
        mbias = jnp.where(jnp.asarray(used)[None, :, None], mbias, 0.0).astype(BF)
        mnew = new_rows(moba_new[S])
        qbd = _qbd(mq[S].reshape(bs, ts, bw), h8, HEAD_DIM, scale)
        o = paged_decode(page_table, qbd, moba_t, layer=l, kv_pair=0, mode="sel", selbias=mbias,
                         blk=MOBA_BLOCK, new=(mnew[..., :bw], mnew[..., bw:], nb8))
        o_d_s = _diag_blocks(o, h8, ts, HEAD_DIM).reshape(ns_tok, bw)

        o_a = jnp.concatenate([o_a_p, o_a_s], axis=0)
        o_b = jnp.concatenate([o_b_p, o_b_s], axis=0)
        o_cmp = jnp.concatenate([o_cmp_p, o_cmp_s], axis=0).reshape(n_tok, h8, HEAD_DIM)
        o_sel = jnp.concatenate([o_sel_p, o_sel_s], axis=0).reshape(n_tok, h8, HEAD_DIM)
        o_win = jnp.concatenate([o_win_p, o_win_s], axis=0).reshape(n_tok, h8, HEAD_DIM)
        o_c = (ngate[..., 0:1] * o_cmp + ngate[..., 1:2] * o_sel + ngate[..., 2:3] * o_win).reshape(n_tok, bw)
        o_d = jnp.concatenate([o_d_p, o_d_s], axis=0)
        merged = merge_branches([o_a, o_b, o_c, o_d], w_branch, glog, d_model, l)
        x = matmul(merged, w_out, res=x, layer=l)

        mem_h = rmsnorm(mem_prompt.reshape(bp * n_mem, d_model), norm_memkv_g[l], BF)
        mem_kv_p = matmul(mem_h, w_mem_kv, layer=l)
        hm = rmsnorm(x, norm_mem_g[l], BF)
        qm = matmul(hm, w_mem_q, layer=l)
        mem_vt = jnp.transpose(mem_kv_p[:, bw:].reshape(bp, n_mem, bw), (0, 2, 1)).astype(BF)
        o_m_p = flash_prompt(qm[P], mem_kv_p[:, :bw].astype(BF), mem_vt, n_batch=bp, t_q=t, t_k=n_mem,
                             n_cols=H_MEM, q_col=lambda c: c, k_col=lambda c: c, v_col=lambda c: c, mode="full",
                             scale=mem_hd ** -0.5, causal=False)
        qm_rows = _pad_rows(qm[S].reshape(bs, ts * H_MEM, mem_hd) * mem_hd ** -0.5, DEC_ROWS).astype(BF)
        o = paged_rows(own_pt, qm_rows, mem_rows, mem_bias, layer=l, v_shift=H_MEM)
        o_m_s = o[:, :ts * H_MEM].reshape(ns_tok, bw)
        x = matmul(jnp.concatenate([o_m_p, o_m_s], axis=0), w_mem_o, res=x, layer=l)

        hf = rmsnorm(x, norm_ffn_g[l], BF)
        w_r = _pad_lanes(jnp.concatenate([w_router_group[l], w_router_expert[l]], axis=1))
        logits = matmul(hf, w_r, tn_target=LANES)
        rows = jnp.arange(n_tok)
        lg = logits[:, :N_GROUPS] + b_router_group[l].astype(F32)
        grp = jnp.argmax(lg, axis=-1).astype(jnp.int32)
        p_grp = jax.nn.softmax(lg, axis=-1)[rows, grp]
        le = (logits[:, N_GROUPS:N_GROUPS + n_exp] + b_router_expert[l].astype(F32)).reshape(n_tok, N_GROUPS, EXPERTS_PER_GROUP)
        p_exp = jax.nn.softmax(le[rows, grp], axis=-1)
        w_top, i_top, p_left = [], [], p_exp
        for _ in range(TOPK_IN_GROUP):
            i_k = jnp.argmax(p_left, axis=-1).astype(jnp.int32)
            w_top.append(jnp.max(p_left, axis=-1))
            i_top.append(i_k)
            p_left = jnp.where(jnp.arange(EXPERTS_PER_GROUP, dtype=jnp.int32)[None, :] == i_k[:, None], -1.0, p_left)
        w_top, i_top = jnp.stack(w_top, axis=-1), jnp.stack(i_top, axis=-1)
        gate = p_grp[:, None] * w_top / jnp.sum(w_top, axis=-1, keepdims=True)
        eid = grp[:, None] * EXPERTS_PER_GROUP + i_top.astype(jnp.int32)
        kk = TOPK_IN_GROUP
        m_asg = n_tok * kk
        flat_e = eid.reshape(m_asg)
        onehot = (flat_e[:, None] == jnp.arange(n_exp, dtype=jnp.int32)[None, :]).astype(F32)
        cb = _tile(m_asg, 128)
        oh3 = onehot.reshape(m_asg // cb, cb, n_exp)
        tri = jnp.asarray(np.tril(np.ones((cb, cb), np.float32), -1))
        within = jnp.einsum("ij,bjk->bik", tri, oh3)
        blk_tot = jnp.sum(oh3, axis=1)
        blk_off = jnp.cumsum(blk_tot, axis=0) - blk_tot
        counts = jnp.sum(blk_tot, axis=0).astype(jnp.int32)
        padded = ((counts + MOE_BLOCK - 1) // MOE_BLOCK) * MOE_BLOCK
        pad_end = jnp.cumsum(padded)
        pad_start = pad_end - padded
        slot = jnp.sum((within + blk_off[:, None, :] + pad_start.astype(F32)[None, None, :]) * oh3, axis=-1)
        dest_flat = slot.reshape(m_asg).astype(jnp.int32)
        n_blk = -(-(m_asg + n_exp * (MOE_BLOCK - 1)) // MOE_BLOCK)
        slot_tok = jnp.zeros((n_blk * MOE_BLOCK,), jnp.int32).at[dest_flat].set(
            jnp.arange(m_asg, dtype=jnp.int32) // kk)
        xs = hf[slot_tok]
        blk_first = jnp.arange(n_blk, dtype=jnp.int32) * MOE_BLOCK
        blk_e = jnp.minimum(jnp.sum((pad_end[None, :] <= blk_first[:, None]).astype(jnp.int32), axis=1),
                            n_exp - 1).astype(jnp.int32)
        ys = moe_experts(xs, blk_e, w_exp_gate, w_exp_up, w_exp_down, l)
        contrib = ys[dest_flat].reshape(n_tok, kk, d_model) * gate[..., None].astype(F32)
        x = x + jnp.sum(contrib, axis=1)

        def pr(a, shape):
            return a[P].reshape((bp, t) + shape)

        def sr(a, shape):
            return a[S].reshape((bs, ts) + shape)

        acc_p[0].append(pr(diff_new, (2, h_diff, 2 * HEAD_DIM)))
        acc_p[1].append(pr(fox_new, (2, h8, HEAD_DIM)))
        acc_p[2].append(pr(logf_new, (h8,)))
        acc_p[3].append(pr(nsa_flat, (4, G_NSA, HEAD_DIM)))
        acc_p[4].append(pr(win_flat, (2, G_NSA, HEAD_DIM))[:, t - min(WINDOW, t):])
        acc_p[5].append(pr(moba_new, (2, h8, HEAD_DIM)))
        acc_p[6].append(mem_kv_p.reshape(bp, n_mem, 2, H_MEM, mem_hd))
        acc_s[0].append(sr(diff_new, (2, h_diff, 2 * HEAD_DIM)))
        acc_s[1].append(sr(fox_new, (2, h8, HEAD_DIM)))
        acc_s[2].append(sr(logf_new, (h8,)))
        acc_s[3].append(sr(nsa_flat, (4, G_NSA, HEAD_DIM)))
        acc_s[4].append(win_state.reshape(bs, wb, 2, G_NSA, HEAD_DIM))
        acc_s[5].append(sr(moba_new, (2, h8, HEAD_DIM)))

    y = rmsnorm(x, norm_final_g, F32)
    y_prompt = y[:np_tok].reshape(bp, t, d_model)
    y_sample = y[np_tok:].reshape(bs, ts, d_model)
    return (y_prompt, y_sample) + tuple(jnp.stack(a) for a in acc_p) + tuple(jnp.stack(a) for a in acc_s)
```

```python
import functools
import math

import numpy as np
import jax
import jax.numpy as jnp
from jax import lax
from jax.experimental import pallas as pl
from jax.experimental.pallas import tpu as pltpu

F32 = jnp.float32
BF = jnp.bfloat16

HEAD_DIM = 64
ROPE_DIMS = HEAD_DIM // 4
ROPE_THETA = 500000.0
N_BRANCH = 4
NORM_EPS = 1e-6
NEG_INF = -1e30
G_NSA = 2
CMP_LEN = 32
CMP_STRIDE = 16
SEL_BLOCK = 64
N_SEL = 16
WINDOW = 512
SEL_BONUS = 1e4
MOBA_BLOCK = 256
MOBA_TOPK = 3
H_MEM = 4
N_GROUPS = 4
EXPERTS_PER_GROUP = 8
TOPK_IN_GROUP = 2
MOE_BLOCK = 128
LANES = 128
NEW_PAD = 128
DEC_ROWS = 32
LOG2E = 1.4426950408889634

VMEM_LIMIT = 56 * 1024 * 1024


def _cparams(*sem):
    return pltpu.CompilerParams(dimension_semantics=sem, vmem_limit_bytes=VMEM_LIMIT)


def _tile(n, target, mult=8):
    best = None
    for t in range(mult, min(n, target) + 1, mult):
        if n % t == 0:
            best = t
    return best if best is not None else n


def _rms_kernel(x_ref, g_ref, o_ref):
    x = x_ref[...].astype(F32)
    r = lax.rsqrt(jnp.mean(x * x, axis=-1, keepdims=True) + NORM_EPS)
    o_ref[...] = ((x * r) * g_ref[...]).astype(o_ref.dtype)


def rmsnorm(x, g, dtype):
    n, d = x.shape
    tm = _tile(n, 512)
    return pl.pallas_call(
        _rms_kernel,
        grid=(n // tm,),
        in_specs=[pl.BlockSpec((tm, d), lambda i: (i, 0)), pl.BlockSpec((1, d), lambda i: (0, 0))],
        out_specs=pl.BlockSpec((tm, d), lambda i: (i, 0)),
        out_shape=jax.ShapeDtypeStruct((n, d), dtype),
        compiler_params=_cparams("parallel"),
        name="rmsnorm",
    )(x, g.reshape(1, d).astype(F32))


def _mm_kernel(a_ref, w_ref, o_ref):
    o_ref[...] = jnp.dot(a_ref[...].astype(BF), w_ref[...].astype(BF),
                         preferred_element_type=F32).astype(o_ref.dtype)


def _mm_res_kernel(a_ref, w_ref, r_ref, o_ref):
    y = jnp.dot(a_ref[...].astype(BF), w_ref[...].astype(BF), preferred_element_type=F32)
    o_ref[...] = (r_ref[...] + y).astype(o_ref.dtype)


def matmul(a, w, res=None, out_dtype=F32, tm_target=640, tn_target=512, layer=None):
    m, k = a.shape
    n = w.shape[-1]
    tm = _tile(m, tm_target)
    tn = _tile(n, tn_target, LANES)
    if layer is None:
        w_spec = pl.BlockSpec((k, tn), lambda i, j: (0, j))
    else:
        w_spec = pl.BlockSpec((None, k, tn), lambda i, j: (layer, 0, j))
    in_specs = [pl.BlockSpec((tm, k), lambda i, j: (i, 0)), w_spec]
    args = [a, w]
    kern = _mm_kernel
    if res is not None:
        in_specs.append(pl.BlockSpec((tm, tn), lambda i, j: (i, j)))
        args.append(res)
        kern = _mm_res_kernel
    return pl.pallas_call(
        kern,
        grid=(m // tm, n // tn),
        in_specs=in_specs,
        out_specs=pl.BlockSpec((tm, tn), lambda i, j: (i, j)),
        out_shape=jax.ShapeDtypeStruct((m, n), out_dtype),
        compiler_params=_cparams("parallel", "arbitrary"),
        name="matmul",
    )(*args)


def _cmp_mlp_kernel(x_ref, w1_ref, w2_ref, o_ref):
    h = jnp.dot(x_ref[...].astype(BF), w1_ref[...].astype(BF), preferred_element_type=F32)
    h = jax.nn.gelu(h)
    o_ref[...] = jnp.dot(h.astype(BF), w2_ref[...].astype(BF), preferred_element_type=F32)


def cmp_mlp(x, w1, w2):
    r, k = x.shape
    f = w1.shape[1]
    d = w2.shape[1]
    tm = _tile(r, 512)
    return pl.pallas_call(
        _cmp_mlp_kernel,
        grid=(r // tm,),
        in_specs=[pl.BlockSpec((tm, k), lambda i: (i, 0)), pl.BlockSpec((k, f), lambda i: (0, 0)),
                  pl.BlockSpec((f, d), lambda i: (0, 0))],
        out_specs=pl.BlockSpec((tm, d), lambda i: (i, 0)),
        out_shape=jax.ShapeDtypeStruct((r, d), F32),
        compiler_params=_cparams("parallel"),
        name="cmp_mlp",
    )(x, w1, w2)


def _cmp_finish_kernel(x_ref, pe_ref, w1_ref, w2_ref, o_ref):
    pe_term = jnp.dot(pe_ref[...].astype(BF), w1_ref[...].astype(BF), preferred_element_type=F32)[0:1]
    h = jax.nn.gelu(x_ref[...] + pe_term)
    o_ref[...] = jnp.dot(h.astype(BF), w2_ref[...].astype(BF), preferred_element_type=F32)


def cmp_finish(x, pe_flat, w1, w2):
    r, f = x.shape
    k = w1.shape[0]
    d = w2.shape[1]
    tm = _tile(r, 1024)
    pe8 = jnp.broadcast_to(pe_flat.reshape(1, k), (8, k))
    return pl.pallas_call(
        _cmp_finish_kernel,
        grid=(r // tm,),
        in_specs=[pl.BlockSpec((tm, f), lambda i: (i, 0)), pl.BlockSpec((8, k), lambda i: (0, 0)),
                  pl.BlockSpec((k, f), lambda i: (0, 0)), pl.BlockSpec((f, d), lambda i: (0, 0))],
        out_specs=pl.BlockSpec((tm, d), lambda i: (i, 0)),
        out_shape=jax.ShapeDtypeStruct((r, d), F32),
        compiler_params=_cparams("parallel"),
        name="cmp_finish",
    )(x, pe8, w1, w2)


def _chunk_proj_kernel(x_ref, w_ref, o_ref, rows_ref, *, pp, rpp):
    n_r = w_ref.shape[1]
    for w in range(2):
        for p in range(pp):
            rows_ref[w, p * rpp:(p + 1) * rpp, :] = x_ref[p, w].T
    n_chunks = pp * rpp // n_r
    for w in range(2):
        acc = None
        for r in range(n_r):
            lhs = rows_ref[w, pl.ds(r, n_chunks, stride=n_r), :].astype(BF)
            t = jnp.dot(lhs, w_ref[w, r], preferred_element_type=F32)
            acc = t if acc is None else acc + t
        o_ref[w] = acc


def nsa_chunk_proj(cache_t, w_chunk, *, layer, pp=16):
    n_pool, c, rpp = cache_t.shape[1], cache_t.shape[3], cache_t.shape[4]
    n_r, n_out = w_chunk.shape[1], w_chunk.shape[3]
    pp = _tile(n_pool, pp, 1)
    cps = pp * rpp // n_r
    return pl.pallas_call(
        functools.partial(_chunk_proj_kernel, pp=pp, rpp=rpp),
        grid=(n_pool // pp,),
        in_specs=[pl.BlockSpec((None, pp, 2, c, rpp), lambda i: (layer, i, 0, 0, 0)),
                  pl.BlockSpec(w_chunk.shape, lambda i: (0, 0, 0, 0))],
        out_specs=pl.BlockSpec((2, cps, n_out), lambda i: (0, i, 0)),
        out_shape=jax.ShapeDtypeStruct((2, n_pool * rpp // n_r, n_out), F32),
        scratch_shapes=[pltpu.VMEM((2, pp * rpp, c), F32)],
        compiler_params=_cparams("parallel"),
        name="nsa_chunk_proj",
    )(cache_t, w_chunk)


def _flash_kernel(*refs, mode, tq, tk, n_k, scale, causal, window, bias, sel, lam_init):
    it = iter(refs)
    q_ref, k_ref, vt_ref = next(it), next(it), next(it)
    cq_ref = ck_ref = sel_ref = e_ref = lam_ref = g_ref = None
    if bias:
        cq_ref, ck_ref = next(it), next(it)
    if sel:
        sel_ref, e_ref = next(it), next(it)
    if mode == "diff":
        lam_ref, g_ref = next(it), next(it)
    o_ref, m_ref, l_ref, acc_ref, s_ref = next(it), next(it), next(it), next(it), next(it)

    n_sub = 1 if mode == "full" else 2
    qi = pl.program_id(2)
    lane = lax.broadcasted_iota(jnp.int32, (1, LANES), 1)
    q = q_ref[...].astype(F32) * (scale * LOG2E)
    if n_sub == 2:
        qs = [jnp.where(lane < HEAD_DIM, q, 0.0).astype(BF), jnp.where(lane >= HEAD_DIM, q, 0.0).astype(BF)]
    else:
        qs = [q.astype(BF)]
    qpos = qi * tq + lax.broadcasted_iota(jnp.int32, (1, tq), 1)

    m_ref[...] = jnp.full(m_ref.shape, NEG_INF, F32)
    l_ref[...] = jnp.zeros(l_ref.shape, F32)
    acc_ref[...] = jnp.zeros(acc_ref.shape, F32)

    if sel:
        sels = [sel_ref[e if sel == "per_head" else 0] for e in range(n_sub)]

    def raw_scores(j):
        kb = k_ref[pl.ds(pl.multiple_of(j * tk, tk), tk), :]
        return [lax.dot_general(kb, qs[e], (((1,), (1,)), ((), ())), preferred_element_type=F32)
                for e in range(n_sub)]

    def park(ss):
        for e in range(n_sub):
            s_ref[e] = ss[e]

    def chunk(j, j_next, masked):
        ss_next = raw_scores(j_next)
        ss = [s_ref[e] for e in range(n_sub)]
        off = pl.multiple_of(j * tk, tk)
        vt = vt_ref[:, pl.ds(off, tk)]
        mask = None
        if masked:
            kpos = off + lax.broadcasted_iota(jnp.int32, (tk, 1), 0)
            mask = kpos <= qpos
            if window:
                mask = jnp.logical_and(mask, qpos - kpos <= WINDOW)
        for e in range(n_sub):
            if bias:
                ss[e] = ss[e] - ck_ref[pl.ds(off, tk), e:e + 1]
            me = mask
            if sel:
                selm = jnp.dot(e_ref[pl.ds(off, tk), :], sels[e], preferred_element_type=F32) > 0.5
                me = selm if mask is None else jnp.logical_and(selm, mask)
            if me is not None:
                ss[e] = jnp.where(me, ss[e], NEG_INF)
        m_prev = [m_ref[e] for e in range(n_sub)]
        m_new, ps = [], []
        for e in range(n_sub):
            top = jnp.max(ss[e], axis=0, keepdims=True)
            if bias:
                top = top + cq_ref[e:e + 1, :]
            m_new.append(jnp.maximum(m_prev[e], top))
            shift = m_new[e] - cq_ref[e:e + 1, :] if bias else m_new[e]
            ps.append(jnp.exp2(ss[e] - shift))
        pvs = [jnp.dot(vt, ps[e].astype(BF), preferred_element_type=F32) for e in range(n_sub)]
        for e in range(n_sub):
            alpha = jnp.exp2(m_prev[e] - m_new[e])
            l_ref[e] = alpha * l_ref[e] + jnp.sum(ps[e], axis=0, keepdims=True)
            acc_ref[e] = alpha * acc_ref[e] + pvs[e]
            m_ref[e] = m_new[e]
        park(ss_next)

    if not causal:
        lo, n_open, hi = 0, n_k, n_k
    elif window:
        lo = jnp.maximum(qi * tq - WINDOW, 0) // tk
        n_open, hi = lo, ((qi + 1) * tq + tk - 1) // tk
    else:
        lo = 0
        n_open = (qi * tq) // tk
        hi = ((qi + 1) * tq + tk - 1) // tk

    def masked_body(j, carry):
        chunk(j, jnp.minimum(j + 1, hi - 1), True)
        return carry

    def open_body(j, carry):
        chunk(j, jnp.minimum(j + 1, hi - 1), False)
        return carry

    park(raw_scores(lo))
    lax.fori_loop(lo, n_open, open_body, 0)
    lax.fori_loop(n_open, hi, masked_body, 0)

    outs = []
    for e in range(n_sub):
        l = l_ref[e]
        outs.append(acc_ref[e] * jnp.where(l > 0.0, 1.0 / l, 0.0))
    if mode == "full":
        ot = outs[0]
    elif mode == "pair":
        sub = lax.broadcasted_iota(jnp.int32, (LANES, 1), 0)
        ot = jnp.where(sub < HEAD_DIM, outs[0], outs[1])
    else:
        ot = outs[0] - lam_ref[0] * outs[1]
    o = ot.T
    if mode == "diff":
        y = o * lax.rsqrt(jnp.mean(o * o, axis=-1, keepdims=True) + NORM_EPS)
        o = (y * g_ref[...]) * (1.0 - lam_init)
    o_ref[...] = o.astype(o_ref.dtype)


def flash_prompt(q, k, vt, *, n_batch, t_q, t_k, n_cols, q_col, k_col, v_col, mode, scale,
                 causal=True, window=False, bias=None, sel=None, sel_mode=None, emat=None,
                 lam=None, subln_g=None, lam_init=0.0, tq=512, tk=512):
    tq = min(tq, t_q)
    tk = min(tk, t_k)
    nq = t_q // tq
    n_k = t_k // tk
    in_specs = [
        pl.BlockSpec((tq, LANES), lambda b, c, i: (b * nq + i, q_col(c))),
        pl.BlockSpec((t_k, LANES), lambda b, c, i: (b, k_col(c))),
        pl.BlockSpec((None, LANES, t_k), lambda b, c, i: (b, v_col(c), 0)),
    ]
    args = [q, k, vt]
    if bias is not None:
        cq, ck = bias
        in_specs += [pl.BlockSpec((None, None, 2, tq), lambda b, c, i: (b, c, 0, i)),
                     pl.BlockSpec((None, None, t_k, 2), lambda b, c, i: (b, c, 0, 0))]
        args += [cq, ck]
    if sel is not None:
        if sel_mode == "per_head":
            in_specs.append(pl.BlockSpec((None, 2, LANES, tq), lambda b, c, i: (b, c, 0, i)))
        else:
            in_specs.append(pl.BlockSpec((None, 1, LANES, tq), lambda b, c, i: (b, c // 2, 0, i)))
        in_specs.append(pl.BlockSpec((t_k, LANES), lambda b, c, i: (0, 0)))
        args += [sel, emat]
    if mode == "diff":
        in_specs += [pl.BlockSpec(memory_space=pltpu.SMEM), pl.BlockSpec((1, LANES), lambda b, c, i: (0, 0))]
        args += [lam.reshape(1).astype(F32), subln_g.reshape(1, LANES).astype(F32)]
    n_sub = 1 if mode == "full" else 2
    kern = functools.partial(_flash_kernel, mode=mode, tq=tq, tk=tk, n_k=n_k, scale=scale, causal=causal,
                             window=window, bias=bias is not None, sel=sel_mode if sel is not None else None,
                             lam_init=lam_init)
    return pl.pallas_call(
        kern,
        grid=(n_batch, n_cols, nq),
        in_specs=in_specs,
        out_specs=pl.BlockSpec((tq, LANES), lambda b, c, i: (b * nq + i, c)),
        out_shape=jax.ShapeDtypeStruct((n_batch * t_q, n_cols * LANES), F32),
        scratch_shapes=[pltpu.VMEM((n_sub, 1, tq), F32), pltpu.VMEM((n_sub, 1, tq), F32),
                        pltpu.VMEM((n_sub, LANES, tq), F32), pltpu.VMEM((n_sub, tk, tq), F32)],
        compiler_params=_cparams("parallel", "parallel", "arbitrary"),
        name="flash_" + mode,
    )(*args)


def _cmp_attn_kernel(q_ref, k_ref, v_ref, ov_ref, o_ref, imp_ref, *, tq, scale):
    qi = pl.program_id(2)
    lane = lax.broadcasted_iota(jnp.int32, (1, LANES), 1)
    n_c = k_ref.shape[0]
    qpos = qi * tq + lax.broadcasted_iota(jnp.int32, (tq, 1), 0)
    ends = lax.broadcasted_iota(jnp.int32, (1, n_c), 1) * CMP_STRIDE + (CMP_LEN - 1)
    mask = ends <= qpos
    kb = k_ref[...].astype(BF)
    vb = v_ref[...].astype(BF)
    ov = ov_ref[...]
    imp = jnp.zeros((tq, LANES), F32)
    halves = []
    for pair in range(2):
        q = q_ref[:, pair * LANES:(pair + 1) * LANES].astype(F32) * scale
        outs = []
        for e in range(2):
            qe = (jnp.where(lane < HEAD_DIM, q, 0.0) if e == 0 else jnp.where(lane >= HEAD_DIM, q, 0.0)).astype(BF)
            s = lax.dot_general(qe, kb, (((1,), (1,)), ((), ())), preferred_element_type=F32)
            s = jnp.where(mask, s, NEG_INF)
            m = jnp.max(s, axis=-1, keepdims=True)
            p = jnp.where(mask, jnp.exp(s - m), 0.0)
            l = jnp.sum(p, axis=-1, keepdims=True)
            p = p * jnp.where(l > 0.0, 1.0 / l, 0.0)
            pb = p.astype(BF)
            outs.append(jnp.dot(pb, vb, preferred_element_type=F32))
            imp = imp + jnp.dot(pb, ov, preferred_element_type=F32)
        halves.append(jnp.where(lane < HEAD_DIM, outs[0], outs[1]))
    o_ref[:, 0:LANES] = halves[0]
    o_ref[:, LANES:2 * LANES] = halves[1]
    imp_ref[...] = imp


def cmp_attn_prompt(q, k_dup, v_dup, ov, *, n_batch, t_q, tq=256):
    tq = min(tq, t_q)
    nq = t_q // tq
    n_c = k_dup.shape[2]
    kern = functools.partial(_cmp_attn_kernel, tq=tq, scale=HEAD_DIM ** -0.5)
    return pl.pallas_call(
        kern,
        grid=(n_batch, G_NSA, nq),
        in_specs=[pl.BlockSpec((tq, 2 * LANES), lambda b, g, i: (b * nq + i, g)),
                  pl.BlockSpec((None, None, n_c, LANES), lambda b, g, i: (b, g, 0, 0)),
                  pl.BlockSpec((None, None, n_c, LANES), lambda b, g, i: (b, g, 0, 0)),
                  pl.BlockSpec((n_c, LANES), lambda b, g, i: (0, 0))],
        out_specs=[pl.BlockSpec((tq, 2 * LANES), lambda b, g, i: (b * nq + i, g)),
                   pl.BlockSpec((None, None, tq, LANES), lambda b, g, i: (b, g, i, 0))],
        out_shape=[jax.ShapeDtypeStruct((n_batch * t_q, G_NSA * 2 * LANES), F32),
                   jax.ShapeDtypeStruct((n_batch, G_NSA, t_q, LANES), F32)],
        compiler_params=_cparams("parallel", "parallel", "arbitrary"),
        name="cmp_attn",
    )(q, k_dup, v_dup, ov)


def _decode_kernel(*refs, has_bias, has_new, emit_p):
    it = iter(refs)
    q_ref, k_ref, v_ref = next(it), next(it), next(it)
    b_ref = next(it) if has_bias else None
    if has_new:
        kn_ref, vn_ref, bn_ref = next(it), next(it), next(it)
    o_ref = next(it)
    p_ref = next(it) if emit_p else None
    m_ref, l_ref, acc_ref = next(it), next(it), next(it)
    j = pl.program_id(1)
    last = pl.num_programs(1) - 1

    @pl.when(j == 0)
    def _():
        m_ref[...] = jnp.full(m_ref.shape, NEG_INF, F32)
        l_ref[...] = jnp.zeros(l_ref.shape, F32)
        acc_ref[...] = jnp.zeros(acc_ref.shape, F32)

    q = q_ref[...]

    def update(kb, vb, b):
        s = lax.dot_general(q, kb.astype(BF), (((1,), (1,)), ((), ())), preferred_element_type=F32)
        if b is not None:
            s = s + b
        m_prev = m_ref[...]
        m_new = jnp.maximum(m_prev, jnp.max(s, axis=-1, keepdims=True))
        alpha = jnp.exp(m_prev - m_new)
        p = jnp.exp(s - m_new)
        l_ref[...] = alpha * l_ref[...] + jnp.sum(p, axis=-1, keepdims=True)
        acc_ref[...] = alpha * acc_ref[...] + jnp.dot(p.astype(BF), vb.astype(BF), preferred_element_type=F32)
        m_ref[...] = m_new
        return p

    p = update(k_ref[...], v_ref[...], b_ref[...] if has_bias else None)

    @pl.when(j == last)
    def _():
        if has_new:
            update(kn_ref[...], vn_ref[...], bn_ref[...])
        inv = 1.0 / l_ref[...]
        o_ref[...] = acc_ref[...] * inv
        if emit_p:
            p_ref[...] = p * inv


def decode_attn(qbd, k, v, *, k_col, v_col, ck, cv, bias=None, new=None, emit_p=False, rk=1024):
    n_b, r = k.shape[0], k.shape[1]
    rk = min(rk, r)
    n_j = r // rk
    assert not emit_p or (n_j == 1 and new is None)
    in_specs = [pl.BlockSpec((None, LANES, ck), lambda b, j: (b, 0, 0)),
                pl.BlockSpec((None, rk, ck), lambda b, j: (b, j, k_col)),
                pl.BlockSpec((None, rk, cv), lambda b, j: (b, j, v_col))]
    args = [qbd, k, v]
    if bias is not None:
        in_specs.append(pl.BlockSpec((None, LANES, rk), lambda b, j: (b, 0, j)))
        args.append(bias)
    if new is not None:
        kn, vn, bn = new
        in_specs += [pl.BlockSpec((None, NEW_PAD, ck), lambda b, j: (b, 0, k_col)),
                     pl.BlockSpec((None, NEW_PAD, cv), lambda b, j: (b, 0, v_col)),
                     pl.BlockSpec((None, LANES, NEW_PAD), lambda b, j: (b, 0, 0))]
        args += [kn, vn, bn]
    out_specs = [pl.BlockSpec((None, LANES, cv), lambda b, j: (b, 0, 0))]
    out_shape = [jax.ShapeDtypeStruct((n_b, LANES, cv), F32)]
    if emit_p:
        out_specs.append(pl.BlockSpec((None, LANES, rk), lambda b, j: (b, 0, 0)))
        out_shape.append(jax.ShapeDtypeStruct((n_b, LANES, r), F32))
    kern = functools.partial(_decode_kernel, has_bias=bias is not None, has_new=new is not None, emit_p=emit_p)
    out = pl.pallas_call(
        kern,
        grid=(n_b, n_j),
        in_specs=in_specs,
        out_specs=out_specs,
        out_shape=out_shape,
        scratch_shapes=[pltpu.VMEM((LANES, 1), F32), pltpu.VMEM((LANES, 1), F32), pltpu.VMEM((LANES, cv), F32)],
        compiler_params=_cparams("parallel", "arbitrary"),
        name="decode_attn",
    )(*args)
    return out if emit_p else out[0]


def _paged_kernel(*refs, pp, n_pages, rpp, mode, blk_shift, has_new):
    it = iter(refs)
    pt_ref, q_ref = next(it), next(it)
    kv_refs = [next(it) for _ in range(pp)]
    b_refs = sel_ref = f_ref = np_ref = None
    if mode == "dense":
        b_refs = [next(it) for _ in range(pp)]
    if mode == "sel":
        sel_ref = next(it)
    if mode == "fox":
        f_ref = next(it)
        np_ref = next(it)
    if has_new:
        kn_ref, vn_ref, bn_ref = next(it), next(it), next(it)
    o_ref, m_ref, l_ref, acc_ref = next(it), next(it), next(it), next(it)
    carry_ref = next(it) if mode == "fox" else None
    step = pl.program_id(1)
    last = pl.num_programs(1) - 1

    @pl.when(step == 0)
    def _():
        m_ref[...] = jnp.full(m_ref.shape, NEG_INF, F32)
        l_ref[...] = jnp.zeros(l_ref.shape, F32)
        acc_ref[...] = jnp.zeros(acc_ref.shape, F32)
        if mode == "fox":
            carry_ref[...] = jnp.zeros(carry_ref.shape, F32)

    q = q_ref[...]
    nr = q.shape[0]
    width = pp * rpp
    kcat = jnp.concatenate([kv_refs[i][0] for i in range(pp)], axis=1).astype(BF)
    s = jnp.dot(q, kcat, preferred_element_type=F32)
    if mode == "dense":
        s = s + jnp.concatenate([b_refs[i][...] for i in range(pp)], axis=1)
    if mode == "sel":
        col = lax.broadcasted_iota(jnp.int32, (1, width), 1)
        rshift = int(math.log2(rpp))
        key = (n_pages - 1 - step * pp - jnp.right_shift(col, rshift)) * rpp + jnp.bitwise_and(col, rpp - 1)
        expand = (lax.broadcasted_iota(jnp.int32, (LANES, width), 0) == jnp.right_shift(key, blk_shift)).astype(BF)
        s = s + jnp.dot(sel_ref[...], expand, preferred_element_type=F32)
    if mode == "fox":
        n_h = f_ref.shape[1]
        lf = [f_ref[pp - 1 - i] for i in range(pp)]
        upper = (lax.broadcasted_iota(jnp.int32, (rpp, rpp), 0) >
                 lax.broadcasted_iota(jnp.int32, (rpp, rpp), 1)).astype(BF)
        lf_all = jnp.concatenate(lf + lf if pp * n_h < 16 else lf, axis=0)
        hi = lf_all.astype(BF)
        lo = (lf_all - hi.astype(F32)).astype(BF)
        suffix = jnp.dot(hi, upper, preferred_element_type=F32) + jnp.dot(lo, upper, preferred_element_type=F32)
        carry = carry_ref[...]
        parts = []
        for i in range(pp):
            parts.append(suffix[i * n_h:(i + 1) * n_h] + carry)
            carry = carry + jnp.sum(lf[i], axis=-1, keepdims=True)
        carry_ref[...] = carry
        b8 = jnp.concatenate(parts, axis=1)
        s = s + (jnp.concatenate([b8] * (nr // n_h), axis=0) + jnp.concatenate([np_ref[...]] * pp, axis=1))

    def accumulate(s, pv_of):
        m_prev = m_ref[...]
        m_new = jnp.maximum(m_prev, jnp.max(s, axis=-1, keepdims=True))
        alpha = jnp.exp(m_prev - m_new)
        p = jnp.exp(s - m_new)
        l_ref[...] = alpha * l_ref[...] + jnp.sum(p, axis=-1, keepdims=True)
        acc_ref[...] = alpha * acc_ref[...] + pv_of(p.astype(BF))
        m_ref[...] = m_new

    vcat = jnp.concatenate([kv_refs[i][1] for i in range(pp)], axis=1).astype(BF)
    accumulate(s, lambda p: lax.dot_general(p, vcat, (((1,), (1,)), ((), ())), preferred_element_type=F32))

    @pl.when(step == last)
    def _():
        if has_new:
            sn = lax.dot_general(q, kn_ref[...].astype(BF), (((1,), (1,)), ((), ())),
                                 preferred_element_type=F32) + bn_ref[...]
            accumulate(sn, lambda p: jnp.dot(p, vn_ref[...].astype(BF), preferred_element_type=F32))
        o_ref[...] = acc_ref[...] * (1.0 / l_ref[...])


def paged_decode(pt, qbd, cache_t, *, layer, kv_pair, pp=8, mode=None, bias=None, selbias=None,
                 blk=None, logf=None, newpre=None, new=None):
    n_b, n_pages = pt.shape
    nr = qbd.shape[1]
    c, rpp = cache_t.shape[3], cache_t.shape[4]
    pp = min(pp, n_pages)
    assert n_pages % pp == 0
    n_steps = n_pages // pp

    def page_map(b, s, pt_ref, *, i):
        return (layer, pt_ref[b, n_pages - 1 - (s * pp + i)], kv_pair, 0, 0)

    in_specs = [pl.BlockSpec((None, nr, c), lambda b, s, pt_ref: (b, 0, 0))]
    args = [qbd]
    for i in range(pp):
        in_specs.append(pl.BlockSpec((None, None, 2, c, rpp), functools.partial(page_map, i=i)))
        args.append(cache_t)
    if mode == "dense":
        shared = bias.shape[0] == 1
        for i in range(pp):
            in_specs.append(pl.BlockSpec(
                (None, nr, rpp),
                functools.partial(lambda b, s, pt_ref, *, i: (0 if shared else b, 0, n_pages - 1 - (s * pp + i)), i=i)))
            args.append(bias)
    if mode == "sel":
        in_specs.append(pl.BlockSpec((None, nr, LANES), lambda b, s, pt_ref: (b, 0, 0)))
        args.append(selbias)
    if mode == "fox":
        n_h = logf.shape[2]
        in_specs.append(pl.BlockSpec((None, pp, n_h, rpp), lambda b, s, pt_ref: (b, n_steps - 1 - s, 0, 0)))
        args.append(logf)
        in_specs.append(pl.BlockSpec((None, nr, LANES), lambda b, s, pt_ref: (b, 0, 0)))
        args.append(newpre)
    if new is not None:
        kn, vn, bn = new
        in_specs += [pl.BlockSpec((None, NEW_PAD, c), lambda b, s, pt_ref: (b, 0, 0)),
                     pl.BlockSpec((None, NEW_PAD, c), lambda b, s, pt_ref: (b, 0, 0)),
                     pl.BlockSpec((None, nr, NEW_PAD), lambda b, s, pt_ref: (b, 0, 0))]
        args += [kn, vn, bn]
    scratch = [pltpu.VMEM((nr, 1), F32), pltpu.VMEM((nr, 1), F32), pltpu.VMEM((nr, c), F32)]
    if mode == "fox":
        scratch.append(pltpu.VMEM((logf.shape[2], 1), F32))
    grid_spec = pltpu.PrefetchScalarGridSpec(
        num_scalar_prefetch=1,
        grid=(n_b, n_steps),
        in_specs=in_specs,
        out_specs=pl.BlockSpec((None, nr, c), lambda b, s, pt_ref: (b, 0, 0)),
        scratch_shapes=scratch,
    )
    kern = functools.partial(_paged_kernel, pp=pp, n_pages=n_pages, rpp=rpp, mode=mode,
                             blk_shift=None if blk is None else int(math.log2(blk)), has_new=new is not None)
    return pl.pallas_call(
        kern,
        grid_spec=grid_spec,
        out_shape=jax.ShapeDtypeStruct((n_b, nr, c), F32),
        compiler_params=_cparams("parallel", "arbitrary"),
        name="paged_" + (mode or "plain"),
    )(pt, *args)


def _rows_kernel(*refs, pp, v_shift, has_new):
    it = iter(refs)
    pt_ref, q_ref = next(it), next(it)
    x_refs = [next(it) for _ in range(pp)]
    b_ref = next(it)
    if has_new:
        xn_ref, bn_ref = next(it), next(it)
    o_ref, m_ref, l_ref, acc_ref = next(it), next(it), next(it), next(it)
    step = pl.program_id(1)
    last = pl.num_programs(1) - 1

    @pl.when(step == 0)
    def _():
        m_ref[...] = jnp.full(m_ref.shape, NEG_INF, F32)
        l_ref[...] = jnp.zeros(l_ref.shape, F32)
        acc_ref[...] = jnp.zeros(acc_ref.shape, F32)

    q = q_ref[...]

    def accumulate(x, bias):
        s = lax.dot_general(q, x, (((1,), (1,)), ((), ())), preferred_element_type=F32) + bias
        m_prev = m_ref[...]
        m_new = jnp.maximum(m_prev, jnp.max(s, axis=-1, keepdims=True))
        alpha = jnp.exp(m_prev - m_new)
        p = jnp.exp(s - m_new)
        l_ref[...] = alpha * l_ref[...] + jnp.sum(p, axis=-1, keepdims=True)
        pv = jnp.dot(pltpu.roll(p, v_shift, 1).astype(BF), x, preferred_element_type=F32)
        acc_ref[...] = alpha * acc_ref[...] + pv
        m_ref[...] = m_new

    accumulate(jnp.concatenate([x_refs[i][...] for i in range(pp)], axis=0).astype(BF), b_ref[...])

    @pl.when(step == last)
    def _():
        if has_new:
            accumulate(xn_ref[...].astype(BF), bn_ref[...])
        o_ref[...] = acc_ref[...] * (1.0 / l_ref[...])


def paged_rows(pt, q, cache_rows, bias, *, layer, v_shift, pp=8, new=None):
    n_b, n_pages = pt.shape
    nr = q.shape[1]
    rows = cache_rows.shape[2]
    pp = min(pp, n_pages)
    assert n_pages % pp == 0
    in_specs = [pl.BlockSpec((None, nr, LANES), lambda b, s, pt_ref: (b, 0, 0))]
    args = [q]
    for i in range(pp):
        in_specs.append(pl.BlockSpec(
            (None, None, rows, LANES),
            functools.partial(lambda b, s, pt_ref, *, i: (layer, pt_ref[b, s * pp + i], 0, 0), i=i)))
        args.append(cache_rows)
    in_specs.append(pl.BlockSpec((nr, pp * rows), lambda b, s, pt_ref: (0, 0)))
    args.append(jnp.concatenate([bias] * pp, axis=1))
    if new is not None:
        in_specs += [pl.BlockSpec((None, NEW_PAD, LANES), lambda b, s, pt_ref: (b, 0, 0)),
                     pl.BlockSpec((nr, NEW_PAD), lambda b, s, pt_ref: (0, 0))]
        args += list(new)
    grid_spec = pltpu.PrefetchScalarGridSpec(
        num_scalar_prefetch=1,
        grid=(n_b, n_pages // pp),
        in_specs=in_specs,
        out_specs=pl.BlockSpec((None, nr, LANES), lambda b, s, pt_ref: (b, 0, 0)),
        scratch_shapes=[pltpu.VMEM((nr, 1), F32), pltpu.VMEM((nr, 1), F32), pltpu.VMEM((nr, LANES), F32)],
    )
    return pl.pallas_call(
        functools.partial(_rows_kernel, pp=pp, v_shift=v_shift, has_new=new is not None),
        grid_spec=grid_spec,
        out_shape=jax.ShapeDtypeStruct((n_b, nr, LANES), F32),
        compiler_params=_cparams("parallel", "arbitrary"),
        name="paged_rows",
    )(pt, *args)


def _block_sum_kernel(*refs, pp, ppb):
    pt_ref = refs[0]
    k_refs = refs[1:1 + pp]
    o_ref = refs[1 + pp]
    ones = jnp.ones((8, k_refs[0].shape[1]), BF)
    rows = []
    for bi in range(pp // ppb):
        acc = None
        for j in range(ppb):
            x = k_refs[bi * ppb + j][...]
            hi = x.astype(BF)
            lo = (x - hi.astype(F32)).astype(BF)
            t = (lax.dot_general(ones, hi, (((1,), (1,)), ((), ())), preferred_element_type=F32) +
                 lax.dot_general(ones, lo, (((1,), (1,)), ((), ())), preferred_element_type=F32))
            acc = t if acc is None else acc + t
        rows.append(acc[0:1])
    o_ref[...] = jnp.concatenate(rows, axis=0)


def page_block_sums(pt, cache_t, *, layer, slot, ppb, pp=16):
    n_b, n_pages = pt.shape
    c, rpp = cache_t.shape[3], cache_t.shape[4]
    pp = min(pp, n_pages)
    assert n_pages % pp == 0 and pp % ppb == 0
    in_specs = []
    for i in range(pp):
        in_specs.append(pl.BlockSpec(
            (None, None, None, c, rpp),
            functools.partial(lambda b, s, pt_ref, *, i: (layer, pt_ref[b, s * pp + i], slot, 0, 0), i=i)))
    grid_spec = pltpu.PrefetchScalarGridSpec(
        num_scalar_prefetch=1,
        grid=(n_b, n_pages // pp),
        in_specs=in_specs,
        out_specs=pl.BlockSpec((None, pp // ppb, c), lambda b, s, pt_ref: (b, s, 0)),
    )
    return pl.pallas_call(
        functools.partial(_block_sum_kernel, pp=pp, ppb=ppb),
        grid_spec=grid_spec,
        out_shape=jax.ShapeDtypeStruct((n_b, n_pages // ppb, c), F32),
        compiler_params=_cparams("parallel", "arbitrary"),
        name="page_block_sums",
    )(pt, *([cache_t] * pp))


def _merge_kernel(a_ref, b_ref, c_ref, d_ref, w_ref, g0_ref, g1_ref, g2_ref, g3_ref, o_ref):
    acc = None
    for n, (br, g) in enumerate(((a_ref, g0_ref), (b_ref, g1_ref), (c_ref, g2_ref), (d_ref, g3_ref))):
        y = jnp.dot(br[...].astype(BF), w_ref[n].astype(BF), preferred_element_type=F32)
        t = jax.nn.sigmoid(g[...].astype(F32)) * y
        acc = t if acc is None else acc + t
    o_ref[...] = acc.astype(o_ref.dtype)


def merge_branches(branches, w_branch, glog, d_model, layer):
    n, bw = branches[0].shape
    tm = _tile(n, 640)
    tn = _tile(d_model, 512, LANES)
    nj = d_model // tn
    in_specs = [pl.BlockSpec((tm, bw), lambda i, j: (i, 0)) for _ in range(N_BRANCH)]
    in_specs.append(pl.BlockSpec((None, N_BRANCH, bw, tn), lambda i, j: (layer, 0, 0, j)))
    for nb in range(N_BRANCH):
        in_specs.append(pl.BlockSpec((tm, tn), functools.partial(lambda i, j, nb: (i, nb * nj + j), nb=nb)))
    return pl.pallas_call(
        _merge_kernel,
        grid=(n // tm, nj),
        in_specs=in_specs,
        out_specs=pl.BlockSpec((tm, tn), lambda i, j: (i, j)),
        out_shape=jax.ShapeDtypeStruct((n, d_model), BF),
        compiler_params=_cparams("parallel", "arbitrary"),
        name="merge",
    )(*branches, w_branch, glog, glog, glog, glog)


def _moe_kernel(be_ref, x_ref, wg_ref, wu_ref, wd_ref, o_ref):
    x = x_ref[...].astype(BF)
    g = jnp.dot(x, wg_ref[...].astype(BF), preferred_element_type=F32)
    u = jnp.dot(x, wu_ref[...].astype(BF), preferred_element_type=F32)
    hid = (jax.nn.silu(g) * u).astype(BF)
    o_ref[...] = jnp.dot(hid, wd_ref[...].astype(BF), preferred_element_type=F32)


def moe_experts(xs, blk_e, w_g, w_u, w_d, layer):
    rows, d = xs.shape
    f = w_g.shape[3]
    n_blk = rows // MOE_BLOCK
    grid_spec = pltpu.PrefetchScalarGridSpec(
        num_scalar_prefetch=1,
        grid=(n_blk,),
        in_specs=[pl.BlockSpec((MOE_BLOCK, d), lambda i, be: (i, 0)),
                  pl.BlockSpec((None, None, d, f), lambda i, be: (layer, be[i], 0, 0)),
                  pl.BlockSpec((None, None, d, f), lambda i, be: (layer, be[i], 0, 0)),
                  pl.BlockSpec((None, None, f, d), lambda i, be: (layer, be[i], 0, 0))],
        out_specs=pl.BlockSpec((MOE_BLOCK, d), lambda i, be: (i, 0)),
    )
    return pl.pallas_call(
        _moe_kernel,
        grid_spec=grid_spec,
        out_shape=jax.ShapeDtypeStruct((rows, d), F32),
        compiler_params=_cparams("arbitrary"),
        name="moe_experts",
    )(blk_e, xs, w_g, w_u, w_d)


def _rope_tables(pos):
    half = ROPE_DIMS // 2
    inv_freq = jnp.exp(jnp.arange(half, dtype=F32) * (-2.0 * math.log(ROPE_THETA) / ROPE_DIMS))
    ang = jnp.asarray(pos, F32)[:, None] * inv_freq[None, :]
    return jnp.cos(ang), jnp.sin(ang)


def _rope(x, cos, sin):
    half = ROPE_DIMS // 2
    c = cos[:, None, :]
    s = sin[:, None, :]
    x1, x2 = x[..., :half], x[..., half:ROPE_DIMS]
    return jnp.concatenate([x1 * c - x2 * s, x2 * c + x1 * s, x[..., ROPE_DIMS:]], axis=-1)


def _overlap_matrix(n_cmp, n_slc):
    st = np.arange(n_cmp) * CMP_STRIDE
    en = st + CMP_LEN
    j = np.arange(n_slc) * SEL_BLOCK
    return ((st[:, None] < j[None, :] + SEL_BLOCK) & (en[:, None] > j[None, :])).astype(np.float32)


def _block_expand_t(n_keys, block):
    return ((np.arange(n_keys) // block)[:, None] == np.arange(LANES)[None, :]).astype(np.float32)


def _top_mask(score, k):
    n = score.shape[-1]
    idx = jnp.arange(n, dtype=jnp.int32)
    mine = score[..., :, None]
    other = score[..., None, :]
    ahead = (other > mine) | ((other == mine) & (idx[None, :] < idx[:, None]))
    return jnp.sum(ahead.astype(jnp.int32), axis=-1) < k


def _nsa_select(imp, q_pos, n_slc):
    own = q_pos // SEL_BLOCK
    j = jnp.arange(n_slc, dtype=jnp.int32)[None, :]
    valid = j <= own[:, None]
    forced = (j == 0) | (j == own[:, None]) | (j == own[:, None] - 1)
    score = jnp.where(valid, imp + jnp.where(forced, SEL_BONUS, 0.0), -SEL_BONUS)
    return _top_mask(score, min(N_SEL, n_slc)) & valid[None, None]


def _moba_select(sc, q_pos, n_blk):
    own = q_pos // MOBA_BLOCK
    past_blk = jnp.arange(n_blk, dtype=jnp.int32)[None, :] < own[:, None]
    sc = jnp.where(past_blk[None, None], sc, NEG_INF)
    return _top_mask(sc, min(MOBA_TOPK, n_blk)) & past_blk[None, None]


def _pad_lanes(x, n=LANES):
    return jnp.pad(x, [(0, 0)] * (x.ndim - 1) + [(0, n - x.shape[-1])])


def _pad_rows(x, n=LANES):
    return jnp.pad(x, [(0, 0)] * (x.ndim - 2) + [(0, n - x.shape[-2]), (0, 0)])


def _compress(k, pe, w1, w2):
    b, tk, g, d = k.shape
    n_cmp = (tk - CMP_LEN) // CMP_STRIDE + 1
    idx = np.arange(n_cmp)[:, None] * CMP_STRIDE + np.arange(CMP_LEN)[None, :]
    blk = k[:, idx] + pe[None, None, :, None, :]
    blk = jnp.transpose(blk, (0, 1, 3, 2, 4)).reshape(b * n_cmp * g, CMP_LEN * d)
    return cmp_mlp(blk, w1, w2).reshape(b, n_cmp, g, d)


def _key_minor(cache):
    l, p, r, a, b, d = cache.shape
    return jnp.transpose(cache, (0, 1, 3, 4, 5, 2)).reshape(l, p, a, b * d, r)


def _qbd(q, n_heads, d_head, scale):
    b, tq, c = q.shape
    qh = q.reshape(b, tq, n_heads, d_head) * scale
    eye = jnp.eye(n_heads, dtype=q.dtype)
    bd = jnp.einsum("bthd,hg->bthgd", qh, eye).reshape(b, tq * n_heads, c)
    return _pad_rows(bd, DEC_ROWS).astype(BF)


def _diag_blocks(o, n_heads, tq, d_out):
    b = o.shape[0]
    o = o[:, :tq * n_heads].reshape(b, tq, n_heads, n_heads, d_out)
    idx = jnp.arange(n_heads)
    return o[:, :, idx, idx]


def _group_blocks(o, tq, hpg):
    b = o.shape[0]
    o = o[:, :tq * G_NSA * hpg].reshape(b, tq, G_NSA, hpg, G_NSA, HEAD_DIM)
    gi = jnp.arange(G_NSA)
    o = o[:, :, gi, :, gi]
    return jnp.transpose(o, (1, 2, 0, 3, 4)).reshape(b, tq, G_NSA * hpg * HEAD_DIM)


def _query_rows(x):
    b, t, h, r = x.shape
    return _pad_rows(x.reshape(b, t * h, r), DEC_ROWS)


def kernel(x_prompt, x_sample, cache_diff_kv, cache_fox_kv, cache_fox_logf, cache_nsa_kv, cache_nsa_win,
           cache_moba_kv, cache_mem_kv, page_table, mem_prompt, norm_mix_g, w_in, b_forget, diff_lambda,
           diff_subln_g, nsa_cmp_pe, nsa_cmp_w1, nsa_cmp_w2, w_branch, w_out, norm_mem_g, norm_memkv_g,
           w_mem_q, w_mem_kv, w_mem_o, norm_ffn_g, w_router_group, b_router_group, w_router_expert,
           b_router_expert, w_exp_gate, w_exp_up, w_exp_down, norm_final_g):
    depth = w_in.shape[0]
    bp, t, d_model = x_prompt.shape
    bs, ts, _ = x_sample.shape
    n_pool, page = cache_diff_kv.shape[1], cache_diff_kv.shape[2]
    n_pages = page_table.shape[1]
    past_len = n_pages * page
    bw = d_model // N_BRANCH
    h_diff = bw // (2 * HEAD_DIM)
    h8 = bw // HEAD_DIM
    hpg = h8 // G_NSA
    gw = G_NSA * HEAD_DIM
    n_mem = mem_prompt.shape[1]
    mem_hd = bw // H_MEM
    n_exp = N_GROUPS * EXPERTS_PER_GROUP
    np_tok = bp * t
    ns_tok = bs * ts
    n_tok = np_tok + ns_tok
    assert h8 * ts <= DEC_ROWS and ts < CMP_STRIDE and past_len % MOBA_BLOCK == 0 and page % CMP_STRIDE == 0
    assert past_len // SEL_BLOCK <= LANES and DEC_ROWS % h8 == 0 and 2 * h_diff == h8 and page == LANES
    scale = HEAD_DIM ** -0.5
    page_table = page_table.astype(jnp.int32)

    pos_p = np.arange(t, dtype=np.int32)
    pos_s = past_len + np.arange(ts, dtype=np.int32)
    pos_all = np.concatenate([np.tile(pos_p, bp), np.tile(pos_s, bs)])
    cos, sin = _rope_tables(pos_all)

    sizes = (bw, bw, bw, bw, bw, bw, h8, bw, 6 * G_NSA * HEAD_DIM, 3 * h8, bw, bw, bw, N_BRANCH * d_model)
    offs = np.concatenate([[0], np.cumsum(sizes)])
    seg = lambda i: (int(offs[i]), int(offs[i + 1]))
    main_ids = (0, 1, 2, 3, 4, 5, 7, 8, 10, 11, 12)
    small_ids = (6, 9)
    n_main = sum(sizes[i] for i in main_ids)
    n_small = sum(sizes[i] for i in small_ids)
    n_main_pad = -(-(n_main + n_small) // 512) * 512
    mo = {}
    o = 0
    for i in main_ids + small_ids:
        mo[i] = o
        o += sizes[i]

    x = jnp.concatenate([x_prompt.reshape(np_tok, d_model), x_sample.reshape(ns_tok, d_model)], axis=0)

    e_moba = jnp.asarray(_block_expand_t(t, MOBA_BLOCK), BF)
    e_nsa = jnp.asarray(_block_expand_t(t, SEL_BLOCK), BF)
    n_cmp_p = (t - CMP_LEN) // CMP_STRIDE + 1
    n_cmp_p_pad = -(-n_cmp_p // LANES) * LANES
    n_slc_p = -(-t // SEL_BLOCK)
    ov_p = np.zeros((n_cmp_p_pad, LANES), np.float32)
    ov_p[:n_cmp_p, :n_slc_p] = _overlap_matrix(n_cmp_p, n_slc_p)
    ov_p = jnp.asarray(ov_p, BF)
    n_mblk_p = -(-t // MOBA_BLOCK)

    tk_s = past_len + ts
    n_cmp_s = (tk_s - CMP_LEN) // CMP_STRIDE + 1
    n_cmp_s_pad = -(-n_cmp_s // LANES) * LANES
    n_slc_s = -(-tk_s // SEL_BLOCK)
    n_slc_past = past_len // SEL_BLOCK
    ov_s = jnp.asarray(_overlap_matrix(n_cmp_s, n_slc_s))
    n_mblk_s = -(-tk_s // MOBA_BLOCK)
    n_mblk_past = past_len // MOBA_BLOCK
    def row_query(n_rows, n_heads):
        return np.minimum(np.arange(n_rows) // n_heads, ts - 1)

    row_t = row_query(DEC_ROWS, h8)
    used = np.arange(DEC_ROWS) < h8 * ts

    def new_bias(n_heads):
        keyi = np.arange(NEW_PAD)
        okm = (keyi[None, :] <= row_query(DEC_ROWS, n_heads)[:, None]) & (keyi[None, :] < ts)
        return np.where(okm, 0.0, NEG_INF).astype(np.float32)

    def new_rows(a):
        return _pad_rows(a.reshape(bs, ts, a.shape[-1]), NEW_PAD)

    fox_t = _key_minor(cache_fox_kv)
    moba_t = _key_minor(cache_moba_kv)
    nsa_t = _key_minor(cache_nsa_kv)
    win_t = _key_minor(cache_nsa_win)
    logf_t = jnp.transpose(cache_fox_logf, (0, 1, 3, 2))
    diff_rows = cache_diff_kv.reshape(depth, n_pool, page * 2 * h_diff, 2 * HEAD_DIM)
    mem_rows = cache_mem_kv.reshape(depth, bs, n_mem * 2 * H_MEM, mem_hd)
    slot_d = np.arange(page * 2 * h_diff) % (2 * h_diff)
    head_d = (np.arange(DEC_ROWS) % h8) // 2
    diff_bias = jnp.asarray(np.where(slot_d[None, :] == head_d[:, None], 0.0, NEG_INF).astype(np.float32))
    slot_n = np.arange(NEW_PAD) % (2 * h_diff)
    tok_n = np.arange(NEW_PAD) // (2 * h_diff)
    diff_ok_n = ((slot_n[None, :] == head_d[:, None]) & (tok_n[None, :] <= row_query(DEC_ROWS, h8)[:, None])
                 & (tok_n[None, :] < ts))
    diff_bias_new = jnp.asarray(np.where(diff_ok_n, 0.0, NEG_INF).astype(np.float32))
    slot_m = np.arange(n_mem * 2 * H_MEM) % (2 * H_MEM)
    mem_bias = jnp.asarray(np.where(slot_m[None, :] == (np.arange(DEC_ROWS) % H_MEM)[:, None], 0.0,
                                    NEG_INF).astype(np.float32))
    comp_eye = jnp.eye(2, dtype=F32)
    wb = cache_nsa_win.shape[2]
    own_pt = jnp.arange(bs, dtype=jnp.int32).reshape(bs, 1)
    cpp = page // CMP_STRIDE

    win_ok = (np.arange(wb)[None, :] >= row_t[:, None] + (wb - WINDOW)) | ~used[:, None]
    win_bias = jnp.asarray(np.where(win_ok, 0.0, NEG_INF).astype(np.float32)).reshape(1, DEC_ROWS, wb)
    ends_s = np.arange(n_cmp_s_pad) * CMP_STRIDE + (CMP_LEN - 1)
    cmp_ok = ((ends_s[None, :] <= (past_len + row_query(LANES, h8))[:, None])
              & (np.arange(n_cmp_s_pad)[None, :] < n_cmp_s))
    cmp_bias = jnp.broadcast_to(jnp.asarray(np.where(cmp_ok, 0.0, NEG_INF).astype(np.float32)),
                                (bs, LANES, n_cmp_s_pad))

    acc_p = [[] for _ in range(7)]
    acc_s = [[] for _ in range(6)]

    for l in range(depth):
        lam_init = 0.8 - 0.6 * math.exp(-0.3 * l)
        lam_p = diff_lambda[l].astype(F32)
        lam = jnp.exp(jnp.sum(lam_p[0] * lam_p[1])) - jnp.exp(jnp.sum(lam_p[2] * lam_p[3])) + lam_init

        wl = w_in[l]
        w_main = jnp.concatenate([wl[:, seg(i)[0]:seg(i)[1]] for i in main_ids + small_ids], axis=1)
        w_main = jnp.pad(w_main, ((0, 0), (0, n_main_pad - n_main - n_small))).astype(BF)
        w_gate = wl[:, seg(13)[0]:seg(13)[1]].astype(BF)
        h = rmsnorm(x, norm_mix_g[l], BF)
        proj = matmul(h, w_main)
        glog = matmul(h, w_gate, tn_target=1024)

        def col(i):
            return proj[:, mo[i]:mo[i] + sizes[i]]

        dq = _rope(col(0).reshape(n_tok, 2 * h_diff, HEAD_DIM), cos, sin).reshape(n_tok, bw)
        dk = _rope(col(1).reshape(n_tok, 2 * h_diff, HEAD_DIM), cos, sin).reshape(n_tok, bw)
        diff_new = jnp.concatenate([dk, col(2)], axis=1)
        fq = col(3)
        fox_new = jnp.concatenate([col(4), col(5)], axis=1)
        logf_new = jax.nn.log_sigmoid(col(6).astype(F32) + b_forget[l].astype(F32))
        nq = _rope(col(7).reshape(n_tok, h8, HEAD_DIM), cos, sin).reshape(n_tok, bw)
        nkv = col(8).reshape(n_tok, 6, G_NSA, HEAD_DIM)
        nsa_new = jnp.stack([nkv[:, 0], nkv[:, 1], _rope(nkv[:, 2], cos, sin), nkv[:, 3]], axis=1)
        win_new = jnp.stack([_rope(nkv[:, 4], cos, sin), nkv[:, 5]], axis=1)
        ngate = jax.nn.sigmoid(col(9).astype(F32)).reshape(n_tok, h8, 3)
        mq = _rope(col(10).reshape(n_tok, h8, HEAD_DIM), cos, sin).reshape(n_tok, bw)
        mk = _rope(col(11).reshape(n_tok, h8, HEAD_DIM), cos, sin).reshape(n_tok, bw)
        moba_new = jnp.concatenate([mk, col(12)], axis=1)
        nsa_flat = nsa_new.reshape(n_tok, 4 * gw)
        win_flat = win_new.reshape(n_tok, 2 * gw)

        P = slice(0, np_tok)

        def keys_bf(a):
            return a.astype(BF)

        def vals_t(a):
            return jnp.transpose(a.reshape(bp, t, a.shape[-1]), (0, 2, 1)).astype(BF)

        o_a_p = flash_prompt(dq[P], keys_bf(dk[P]), vals_t(col(2)[P]), n_batch=bp, t_q=t, t_k=t, n_cols=h_diff,
                             q_col=lambda c: c, k_col=lambda c: c, v_col=lambda c: c, mode="diff",
                             scale=scale, lam=lam, subln_g=diff_subln_g[l], lam_init=lam_init)
        logf_p = logf_new[P].reshape(bp, t, h8)
        c_all = (jnp.cumsum(logf_p, axis=1) * LOG2E).reshape(bp, t, h8 // 2, 2)
        cq = jnp.transpose(c_all, (0, 2, 3, 1))
        ck = jnp.transpose(c_all, (0, 2, 1, 3))
        o_b_p = flash_prompt(fq[P], keys_bf(col(4)[P]), vals_t(col(5)[P]), n_batch=bp, t_q=t, t_k=t, n_cols=h8 // 2,
                             q_col=lambda c: c, k_col=lambda c: c, v_col=lambda c: c, mode="pair",
                             scale=scale, bias=(cq, ck))
        nsa_p = nsa_new[P].reshape(bp, t, 4, G_NSA, HEAD_DIM)
        k_cmp = _compress(nsa_p[:, :, 0], nsa_cmp_pe[l, 0], nsa_cmp_w1[l, 0], nsa_cmp_w2[l, 0])
        v_cmp = _compress(nsa_p[:, :, 1], nsa_cmp_pe[l, 1], nsa_cmp_w1[l, 1], nsa_cmp_w2[l, 1])

        def dup_cmp(a):
            a = jnp.transpose(a, (0, 2, 1, 3))
            a = jnp.concatenate([a, a], axis=-1)
            return jnp.pad(a, ((0, 0), (0, 0), (0, n_cmp_p_pad - n_cmp_p), (0, 0)))

        o_cmp_p, imp_p = cmp_attn_prompt(nq[P], dup_cmp(k_cmp), dup_cmp(v_cmp), ov_p, n_batch=bp, t_q=t)
        sel_nsa = _nsa_select(imp_p[..., :n_slc_p], jnp.asarray(pos_p), n_slc_p)
        sel_nsa = _pad_rows(jnp.transpose(sel_nsa, (0, 1, 3, 2)).astype(BF))

        def dup_g(a):
            return jnp.concatenate([a, a], axis=-1).reshape(a.shape[0], G_NSA * LANES)

        nsa_pt = nsa_new[P]
        ppg = hpg // 2
        o_sel_p = flash_prompt(nq[P], keys_bf(dup_g(nsa_pt[:, 2])), vals_t(dup_g(nsa_pt[:, 3])), n_batch=bp, t_q=t,
                               t_k=t, n_cols=h8 // 2, q_col=lambda c: c, k_col=lambda c: c // ppg,
                               v_col=lambda c: c // ppg, mode="pair", scale=scale, sel=sel_nsa,
                               sel_mode="per_group", emat=e_nsa)
        win_p = win_new[P]
        o_win_p = flash_prompt(nq[P], keys_bf(dup_g(win_p[:, 0])), vals_t(dup_g(win_p[:, 1])), n_batch=bp, t_q=t,
                               t_k=t, n_cols=h8 // 2, q_col=lambda c: c, k_col=lambda c: c // ppg,
                               v_col=lambda c: c // ppg, mode="pair", scale=scale, window=True)
        mk_p = mk[P].reshape(bp, n_mblk_p, MOBA_BLOCK, h8, HEAD_DIM)
        k_mean_p = jnp.transpose(jnp.mean(mk_p.astype(F32), axis=2), (0, 2, 1, 3))
        sc_p = jnp.einsum("bqhd,bhnd->bhqn", mq[P].reshape(bp, t, h8, HEAD_DIM).astype(F32), k_mean_p)
        sel_moba = _moba_select(sc_p, jnp.asarray(pos_p), n_mblk_p)
        own_p = jnp.asarray(pos_p // MOBA_BLOCK)
        sel_moba = sel_moba | (jnp.arange(n_mblk_p)[None, :] == own_p[:, None])[None, None]
        sel_moba = _pad_rows(jnp.transpose(sel_moba, (0, 1, 3, 2)).astype(BF))
        o_d_p = flash_prompt(mq[P], keys_bf(mk[P]), vals_t(col(12)[P]), n_batch=bp, t_q=t, t_k=t, n_cols=h8 // 2,
                             q_col=lambda c: c, k_col=lambda c: c, v_col=lambda c: c, mode="pair",
                             scale=scale, sel=sel_moba, sel_mode="per_head", emat=e_moba)

        S = slice(np_tok, n_tok)
        nb8 = jnp.broadcast_to(jnp.asarray(new_bias(h8)), (bs, DEC_ROWS, NEW_PAD))
        dq5 = dq[S].reshape(bs, ts, h_diff, 2, HEAD_DIM) * scale
        q_rows = jnp.einsum("bthcd,ce->bthced", dq5, comp_eye).reshape(bs, ts * h8, 2 * HEAD_DIM)
        q_rows = _pad_rows(q_rows, DEC_ROWS).astype(BF)
        x_new = _pad_rows(diff_new[S].reshape(bs, ts * 2 * h_diff, 2 * HEAD_DIM), NEW_PAD)
        o = paged_rows(page_table, q_rows, diff_rows, diff_bias, layer=l, v_shift=h_diff,
                       new=(x_new, diff_bias_new))
        o = o[:, :ts * h8].reshape(bs, ts, h_diff, 2, 2 * HEAD_DIM)
        o = o[:, :, :, 0] - lam * o[:, :, :, 1]
        y = o * lax.rsqrt(jnp.mean(o * o, axis=-1, keepdims=True) + NORM_EPS)
        o_a_s = ((y * diff_subln_g[l].astype(F32)) * (1.0 - lam_init)).reshape(ns_tok, bw)

        logf_s = logf_new[S].reshape(bs, ts, h8)
        pre_new = jnp.cumsum(logf_s, axis=1)
        newpre = jnp.broadcast_to(_query_rows(pre_new[..., None]), (bs, DEC_ROWS, LANES))
        d_new = pre_new[:, :, None, :] - pre_new[:, None, :, :]
        d_new = _pad_lanes(_query_rows(jnp.transpose(d_new, (0, 1, 3, 2))), NEW_PAD)
        fnew = new_rows(fox_new[S])
        qbd = _qbd(fq[S].reshape(bs, ts, bw), h8, HEAD_DIM, scale)
        o = paged_decode(page_table, qbd, fox_t, layer=l, kv_pair=0, mode="fox", logf=logf_t[l][page_table],
                         newpre=newpre, new=(fnew[..., :bw], fnew[..., bw:], d_new + nb8))
        o_b_s = _diag_blocks(o, h8, ts, HEAD_DIM).reshape(ns_tok, bw)

        kv_cmp_s = []
        hid = nsa_cmp_w1.shape[3]
        w_rows = jnp.transpose(nsa_cmp_w1[l].reshape(2, 2, CMP_STRIDE, HEAD_DIM, hid), (0, 2, 3, 1, 4))
        w_rows = w_rows.reshape(2, CMP_STRIDE, HEAD_DIM, 2 * hid)
        w_chunk = jnp.einsum("wrdc,gk->wrgdkc", w_rows, jnp.eye(G_NSA, dtype=F32))
        w_chunk = w_chunk.reshape(2, CMP_STRIDE, gw, G_NSA * 2 * hid).astype(BF)
        ab_all = nsa_chunk_proj(nsa_t, w_chunk, layer=l)
        for w in range(2):
            w1 = nsa_cmp_w1[l, w]
            ab = ab_all[w]
            ab = ab.reshape(n_pool, cpp, G_NSA, 2 * hid)[page_table].reshape(bs, n_pages * cpp, G_NSA, 2 * hid)
            pre = ab[:, :n_cmp_s, :, :hid] + ab[:, 1:n_cmp_s + 1, :, hid:]
            kv = cmp_finish(pre.reshape(bs * n_cmp_s * G_NSA, hid), nsa_cmp_pe[l, w].reshape(-1), w1,
                            nsa_cmp_w2[l, w])
            kv_cmp_s.append(kv.reshape(bs, n_cmp_s, gw))
        kv_cmp_s = jnp.pad(jnp.concatenate(kv_cmp_s, axis=-1), ((0, 0), (0, n_cmp_s_pad - n_cmp_s), (0, 0)))
        qbd_n = _qbd(nq[S].reshape(bs, ts, bw), h8, HEAD_DIM, scale)
        qbd_g = qbd_n.reshape(bs, DEC_ROWS, G_NSA, hpg, HEAD_DIM).sum(axis=3).reshape(bs, DEC_ROWS, gw)
        o, p_c = decode_attn(_pad_rows(qbd_g, LANES), kv_cmp_s, kv_cmp_s, k_col=0, v_col=1, ck=gw, cv=gw,
                             bias=cmp_bias, emit_p=True, rk=n_cmp_s_pad)
        o_cmp_s = _group_blocks(o, ts, hpg).reshape(ns_tok, bw)
        p_c = p_c[:, :h8 * ts, :n_cmp_s].reshape(bs, ts, G_NSA, hpg, n_cmp_s)
        imp_s = jnp.einsum("bqgmn,ns->bgqs", p_c, ov_s)
        sel_s = _nsa_select(imp_s, jnp.asarray(pos_s), n_slc_s)
        sel_rows = jnp.broadcast_to(jnp.transpose(sel_s, (0, 2, 1, 3))[:, :, :, None, :n_slc_past],
                                    (bs, ts, G_NSA, hpg, n_slc_past)).reshape(bs, ts, h8, n_slc_past)
        selbias = _pad_lanes(jnp.where(_query_rows(sel_rows.astype(F32)) > 0.5, 0.0, NEG_INF))
        selbias = jnp.where(jnp.asarray(used)[None, :, None], selbias, 0.0).astype(BF)
        nnew = new_rows(nsa_flat[S])
        o = paged_decode(page_table, qbd_g, nsa_t, layer=l, kv_pair=1, mode="sel", selbias=selbias,
                         blk=SEL_BLOCK, new=(nnew[..., 2 * gw:3 * gw], nnew[..., 3 * gw:], nb8))
        o_sel_s = _group_blocks(o, ts, hpg).reshape(ns_tok, bw)
        wnew = new_rows(win_flat[S])
        o = paged_decode(own_pt, qbd_g, win_t, layer=l, kv_pair=0, mode="dense", bias=win_bias,
                         new=(wnew[..., :gw], wnew[..., gw:], nb8))
        o_win_s = _group_blocks(o, ts, hpg).reshape(ns_tok, bw)
        wbuf = cache_nsa_win[l].reshape(bs, wb, 2 * gw)
        win_state = jnp.concatenate([wbuf, win_flat[S].reshape(bs, ts, 2 * gw)], axis=1)[:, ts:]

        k_sum = page_block_sums(page_table, moba_t, layer=l, slot=0, ppb=MOBA_BLOCK // page)
        k_mean_s = (k_sum / MOBA_BLOCK).reshape(bs, n_mblk_past, h8, HEAD_DIM)
        k_mean_s = jnp.pad(jnp.transpose(k_mean_s, (0, 2, 1, 3)), ((0, 0), (0, 0), (0, n_mblk_s - n_mblk_past), (0, 0)))
        sc_s = jnp.einsum("bqhd,bhnd->bhqn", mq[S].reshape(bs, ts, h8, HEAD_DIM).astype(F32), k_mean_s)
        selm_s = _moba_select(sc_s, jnp.asarray(pos_s), n_mblk_s)
        selm_rows = _query_rows(jnp.transpose(selm_s[..., :n_mblk_past], (0, 2, 1, 3)).astype(F32))
        mbias = _pad_lanes(jnp.where(selm_rows > 0.5, 0.0, NEG_INF))
        mbias = jnp.where(jnp.asarray(used)[None, :, None], mbias, 0.0).astype(BF)
        mnew = new_rows(moba_new[S])
        qbd = _qbd(mq[S].reshape(bs, ts, bw), h8, HEAD_DIM, scale)
        o = paged_decode(page_table, qbd, moba_t, layer=l, kv_pair=0, mode="sel", selbias=mbias,
                         blk=MOBA_BLOCK, new=(mnew[..., :bw], mnew[..., bw:], nb8))
        o_d_s = _diag_blocks(o, h8, ts, HEAD_DIM).reshape(ns_tok, bw)

        o_a = jnp.concatenate([o_a_p, o_a_s], axis=0)
        o_b = jnp.concatenate([o_b_p, o_b_s], axis=0)
        o_cmp = jnp.concatenate([o_cmp_p, o_cmp_s], axis=0).reshape(n_tok, h8, HEAD_DIM)
        o_sel = jnp.concatenate([o_sel_p, o_sel_s], axis=0).reshape(n_tok, h8, HEAD_DIM)
        o_win = jnp.concatenate([o_win_p, o_win_s], axis=0).reshape(n_tok, h8, HEAD_DIM)
        o_c = (ngate[..., 0:1] * o_cmp + ngate[..., 1:2] * o_sel + ngate[..., 2:3] * o_win).reshape(n_tok, bw)
        o_d = jnp.concatenate([o_d_p, o_d_s], axis=0)
        merged = merge_branches([o_a, o_b, o_c, o_d], w_branch, glog, d_model, l)
        x = matmul(merged, w_out, res=x, layer=l)

        mem_h = rmsnorm(mem_prompt.reshape(bp * n_mem, d_model), norm_memkv_g[l], BF)
        mem_kv_p = matmul(mem_h, w_mem_kv, layer=l)
        hm = rmsnorm(x, norm_mem_g[l], BF)
        qm = matmul(hm, w_mem_q, layer=l)
        mem_vt = jnp.transpose(mem_kv_p[:, bw:].reshape(bp, n_mem, bw), (0, 2, 1)).astype(BF)
        o_m_p = flash_prompt(qm[P], mem_kv_p[:, :bw].astype(BF), mem_vt, n_batch=bp, t_q=t, t_k=n_mem,
                             n_cols=H_MEM, q_col=lambda c: c, k_col=lambda c: c, v_col=lambda c: c, mode="full",
                             scale=mem_hd ** -0.5, causal=False)
        qm_rows = _pad_rows(qm[S].reshape(bs, ts * H_MEM, mem_hd) * mem_hd ** -0.5, DEC_ROWS).astype(BF)
        o = paged_rows(own_pt, qm_rows, mem_rows, mem_bias, layer=l, v_shift=H_MEM)
        o_m_s = o[:, :ts * H_MEM].reshape(ns_tok, bw)
        x = matmul(jnp.concatenate([o_m_p, o_m_s], axis=0), w_mem_o, res=x, layer=l)

        hf = rmsnorm(x, norm_ffn_g[l], BF)
        w_r = _pad_lanes(jnp.concatenate([w_router_group[l], w_router_expert[l]], axis=1))
        logits = matmul(hf, w_r, tn_target=LANES)
        rows = jnp.arange(n_tok)
        lg = logits[:, :N_GROUPS] + b_router_group[l].astype(F32)
        grp = jnp.argmax(lg, axis=-1).astype(jnp.int32)
        p_grp = jax.nn.softmax(lg, axis=-1)[rows, grp]
        le = (logits[:, N_GROUPS:N_GROUPS + n_exp] + b_router_expert[l].astype(F32)).reshape(n_tok, N_GROUPS, EXPERTS_PER_GROUP)
        p_exp = jax.nn.softmax(le[rows, grp], axis=-1)
        w_top, i_top, p_left = [], [], p_exp
        for _ in range(TOPK_IN_GROUP):
            i_k = jnp.argmax(p_left, axis=-1).astype(jnp.int32)
            w_top.append(jnp.max(p_left, axis=-1))
            i_top.append(i_k)
            p_left = jnp.where(jnp.arange(EXPERTS_PER_GROUP, dtype=jnp.int32)[None, :] == i_k[:, None], -1.0, p_left)
        w_top, i_top = jnp.stack(w_top, axis=-1), jnp.stack(i_top, axis=-1)
        gate = p_grp[:, None] * w_top / jnp.sum(w_top, axis=-1, keepdims=True)
        eid = grp[:, None] * EXPERTS_PER_GROUP + i_top.astype(jnp.int32)
        kk = TOPK_IN_GROUP
        m_asg = n_tok * kk
        flat_e = eid.reshape(m_asg)
        onehot = (flat_e[:, None] == jnp.arange(n_exp, dtype=jnp.int32)[None, :]).astype(F32)
        cb = _tile(m_asg, 128)
        oh3 = onehot.reshape(m_asg // cb, cb, n_exp)
        tri = jnp.asarray(np.tril(np.ones((cb, cb), np.float32), -1))
        within = jnp.einsum("ij,bjk->bik", tri, oh3)
        blk_tot = jnp.sum(oh3, axis=1)
        blk_off = jnp.cumsum(blk_tot, axis=0) - blk_tot
        counts = jnp.sum(blk_tot, axis=0).astype(jnp.int32)
        padded = ((counts + MOE_BLOCK - 1) // MOE_BLOCK) * MOE_BLOCK
        pad_end = jnp.cumsum(padded)
        pad_start = pad_end - padded
        slot = jnp.sum((within + blk_off[:, None, :] + pad_start.astype(F32)[None, None, :]) * oh3, axis=-1)
        dest_flat = slot.reshape(m_asg).astype(jnp.int32)
        n_blk = -(-(m_asg + n_exp * (MOE_BLOCK - 1)) // MOE_BLOCK)
        slot_tok = jnp.zeros((n_blk * MOE_BLOCK,), jnp.int32).at[dest_flat].set(
            jnp.arange(m_asg, dtype=jnp.int32) // kk)
        xs = hf[slot_tok]
        blk_first = jnp.arange(n_blk, dtype=jnp.int32) * MOE_BLOCK
        blk_e = jnp.minimum(jnp.sum((pad_end[None, :] <= blk_first[:, None]).astype(jnp.int32), axis=1),
                            n_exp - 1).astype(jnp.int32)
        ys = moe_experts(xs, blk_e, w_exp_gate, w_exp_up, w_exp_down, l)
        contrib = ys[dest_flat].reshape(n_tok, kk, d_model) * gate[..., None].astype(F32)
        x = x + jnp.sum(contrib, axis=1)

        def pr(a, shape):
            return a[P].reshape((bp, t) + shape)

        def sr(a, shape):
            return a[S].reshape((bs, ts) + shape)

        acc_p[0].append(pr(diff_new, (2, h_diff, 2 * HEAD_DIM)))
        acc_p[1].append(pr(fox_new, (2, h8, HEAD_DIM)))
        acc_p[2].append(pr(logf_new, (h8,)))
        acc_p[3].append(pr(nsa_flat, (4, G_NSA, HEAD_DIM)))
        acc_p[4].append(pr(win_flat, (2, G_NSA, HEAD_DIM))[:, t - min(WINDOW, t):])
        acc_p[5].append(pr(moba_new, (2, h8, HEAD_DIM)))
        acc_p[6].append(mem_kv_p.reshape(bp, n_mem, 2, H_MEM, mem_hd))
        acc_s[0].append(sr(diff_new, (2, h_diff, 2 * HEAD_DIM)))
        acc_s[1].append(sr(fox_new, (2, h8, HEAD_DIM)))
        acc_s[2].append(sr(logf_new, (h8,)))
        acc_s[3].append(sr(nsa_flat, (4, G_NSA, HEAD_DIM)))
        acc_s[4].append(win_state.reshape(bs, wb, 2, G_NSA, HEAD_DIM))
        acc_s[5].append(sr(moba_new, (2, h8, HEAD_DIM)))

    y = rmsnorm(x, norm_final_g, F32)
    y_prompt = y[:np_tok].reshape(bp, t, d_model)
    y_sample = y[np_tok:].reshape(bs, ts, d_model)
    return (y_prompt, y_sample) + tuple(jnp.stack(a) for a in acc_p) + tuple(jnp.stack(a) for a in acc_s)
```

```python
import functools
import math

import numpy as np
import jax
import jax.numpy as jnp
from jax import lax
from jax.experimental import pallas as pl
from jax.experimental.pallas import tpu as pltpu

F32 = jnp.float32
BF = jnp.bfloat16

HEAD_DIM = 64
ROPE_DIMS = HEAD_DIM // 4
ROPE_THETA = 500000.0
N_BRANCH = 4
NORM_EPS = 1e-6
NEG_INF = -1e30
G_NSA = 2
CMP_LEN = 32
CMP_STRIDE = 16
SEL_BLOCK = 64
N_SEL = 16
WINDOW = 512
SEL_BONUS = 1e4
MOBA_BLOCK = 256
MOBA_TOPK = 3
H_MEM = 4
N_GROUPS = 4
EXPERTS_PER_GROUP = 8
TOPK_IN_GROUP = 2
MOE_BLOCK = 128
LANES = 128
NEW_PAD = 128
DEC_ROWS = 32
LOG2E = 1.4426950408889634

VMEM_LIMIT = 56 * 1024 * 1024


def _cparams(*sem):
    return pltpu.CompilerParams(dimension_semantics=sem, vmem_limit_bytes=VMEM_LIMIT)


def _tile(n, target, mult=8):
    best = None
    for t in range(mult, min(n, target) + 1, mult):
        if n % t == 0:
            best = t
    return best if best is not None else n


def _rms_kernel(x_ref, g_ref, o_ref):
    x = x_ref[...].astype(F32)
    r = lax.rsqrt(jnp.mean(x * x, axis=-1, keepdims=True) + NORM_EPS)
    o_ref[...] = ((x * r) * g_ref[...]).astype(o_ref.dtype)


def rmsnorm(x, g, dtype):
    n, d = x.shape
    tm = _tile(n, 512)
    return pl.pallas_call(
        _rms_kernel,
        grid=(n // tm,),
        in_specs=[pl.BlockSpec((tm, d), lambda i: (i, 0)), pl.BlockSpec((1, d), lambda i: (0, 0))],
        out_specs=pl.BlockSpec((tm, d), lambda i: (i, 0)),
        out_shape=jax.ShapeDtypeStruct((n, d), dtype),
        compiler_params=_cparams("parallel"),
        name="rmsnorm",
    )(x, g.reshape(1, d).astype(F32))


def _mm_kernel(a_ref, w_ref, o_ref):
    o_ref[...] = jnp.dot(a_ref[...].astype(BF), w_ref[...].astype(BF),
                         preferred_element_type=F32).astype(o_ref.dtype)


def _mm_res_kernel(a_ref, w_ref, r_ref, o_ref):
    y = jnp.dot(a_ref[...].astype(BF), w_ref[...].astype(BF), preferred_element_type=F32)
    o_ref[...] = (r_ref[...] + y).astype(o_ref.dtype)


def matmul(a, w, res=None, out_dtype=F32, tm_target=640, tn_target=512, layer=None):
    m, k = a.shape
    n = w.shape[-1]
    tm = _tile(m, tm_target)
    tn = _tile(n, tn_target, LANES)
    if layer is None:
        w_spec = pl.BlockSpec((k, tn), lambda i, j: (0, j))
    else:
        w_spec = pl.BlockSpec((None, k, tn), lambda i, j: (layer, 0, j))
    in_specs = [pl.BlockSpec((tm, k), lambda i, j: (i, 0)), w_spec]
    args = [a, w]
    kern = _mm_kernel
    if res is not None:
        in_specs.append(pl.BlockSpec((tm, tn), lambda i, j: (i, j)))
        args.append(res)
        kern = _mm_res_kernel
    return pl.pallas_call(
        kern,
        grid=(m // tm, n // tn),
        in_specs=in_specs,
        out_specs=pl.BlockSpec((tm, tn), lambda i, j: (i, j)),
        out_shape=jax.ShapeDtypeStruct((m, n), out_dtype),
        compiler_params=_cparams("parallel", "arbitrary"),
        name="matmul",
    )(*args)


def _cmp_mlp_kernel(x_ref, w1_ref, w2_ref, o_ref):
    h = jnp.dot(x_ref[...].astype(BF), w1_ref[...].astype(BF), preferred_element_type=F32)
    h = jax.nn.gelu(h)
    o_ref[...] = jnp.dot(h.astype(BF), w2_ref[...].astype(BF), preferred_element_type=F32)


def cmp_mlp(x, w1, w2):
    r, k = x.shape
    f = w1.shape[1]
    d = w2.shape[1]
    tm = _tile(r, 512)
    return pl.pallas_call(
        _cmp_mlp_kernel,
        grid=(r // tm,),
        in_specs=[pl.BlockSpec((tm, k), lambda i: (i, 0)), pl.BlockSpec((k, f), lambda i: (0, 0)),
                  pl.BlockSpec((f, d), lambda i: (0, 0))],
        out_specs=pl.BlockSpec((tm, d), lambda i: (i, 0)),
        out_shape=jax.ShapeDtypeStruct((r, d), F32),
        compiler_params=_cparams("parallel"),
        name="cmp_mlp",
    )(x, w1, w2)


def _cmp_finish_kernel(x_ref, pe_ref, w1_ref, w2_ref, o_ref):
    pe_term = jnp.dot(pe_ref[...].astype(BF), w1_ref[...].astype(BF), preferred_element_type=F32)[0:1]
    h = jax.nn.gelu(x_ref[...] + pe_term)
    o_ref[...] = jnp.dot(h.astype(BF), w2_ref[...].astype(BF), preferred_element_type=F32)


def cmp_finish(x, pe_flat, w1, w2):
    r, f = x.shape
    k = w1.shape[0]
    d = w2.shape[1]
    tm = _tile(r, 1024)
    pe8 = jnp.broadcast_to(pe_flat.reshape(1, k), (8, k))
    return pl.pallas_call(
        _cmp_finish_kernel,
        grid=(r // tm,),
        in_specs=[pl.BlockSpec((tm, f), lambda i: (i, 0)), pl.BlockSpec((8, k), lambda i: (0, 0)),
                  pl.BlockSpec((k, f), lambda i: (0, 0)), pl.BlockSpec((f, d), lambda i: (0, 0))],
        out_specs=pl.BlockSpec((tm, d), lambda i: (i, 0)),
        out_shape=jax.ShapeDtypeStruct((r, d), F32),
        compiler_params=_cparams("parallel"),
        name="cmp_finish",
    )(x, pe8, w1, w2)


def _chunk_proj_kernel(x_ref, w_ref, o_ref, rows_ref, *, pp, rpp):
    n_r = w_ref.shape[1]
    for w in range(2):
        for p in range(pp):
            rows_ref[w, p * rpp:(p + 1) * rpp, :] = x_ref[p, w].T
    n_chunks = pp * rpp // n_r
    for w in range(2):
        acc = None
        for r in range(n_r):
            lhs = rows_ref[w, pl.ds(r, n_chunks, stride=n_r), :].astype(BF)
            t = jnp.dot(lhs, w_ref[w, r], preferred_element_type=F32)
            acc = t if acc is None else acc + t
        o_ref[w] = acc


def nsa_chunk_proj(cache_t, w_chunk, *, layer, pp=16):
    n_pool, c, rpp = cache_t.shape[1], cache_t.shape[3], cache_t.shape[4]
    n_r, n_out = w_chunk.shape[1], w_chunk.shape[3]
    pp = _tile(n_pool, pp, 1)
    cps = pp * rpp // n_r
    return pl.pallas_call(
        functools.partial(_chunk_proj_kernel, pp=pp, rpp=rpp),
        grid=(n_pool // pp,),
        in_specs=[pl.BlockSpec((None, pp, 2, c, rpp), lambda i: (layer, i, 0, 0, 0)),
                  pl.BlockSpec(w_chunk.shape, lambda i: (0, 0, 0, 0))],
        out_specs=pl.BlockSpec((2, cps, n_out), lambda i: (0, i, 0)),
        out_shape=jax.ShapeDtypeStruct((2, n_pool * rpp // n_r, n_out), F32),
        scratch_shapes=[pltpu.VMEM((2, pp * rpp, c), F32)],
        compiler_params=_cparams("parallel"),
        name="nsa_chunk_proj",
    )(cache_t, w_chunk)


def _flash_kernel(*refs, mode, tq, tk, n_k, scale, causal, window, bias, sel, lam_init):
    it = iter(refs)
    q_ref, k_ref, vt_ref = next(it), next(it), next(it)
    cq_ref = ck_ref = sel_ref = e_ref = lam_ref = g_ref = None
    if bias:
        cq_ref, ck_ref = next(it), next(it)
    if sel:
        sel_ref, e_ref = next(it), next(it)
    if mode == "diff":
        lam_ref, g_ref = next(it), next(it)
    o_ref, m_ref, l_ref, acc_ref, s_ref = next(it), next(it), next(it), next(it), next(it)

    n_sub = 1 if mode == "full" else 2
    qi = pl.program_id(2)
    lane = lax.broadcasted_iota(jnp.int32, (1, LANES), 1)
    q = q_ref[...].astype(F32) * (scale * LOG2E)
    if n_sub == 2:
        qs = [jnp.where(lane < HEAD_DIM, q, 0.0).astype(BF), jnp.where(lane >= HEAD_DIM, q, 0.0).astype(BF)]
    else:
        qs = [q.astype(BF)]
    qpos = qi * tq + lax.broadcasted_iota(jnp.int32, (1, tq), 1)

    m_ref[...] = jnp.full(m_ref.shape, NEG_INF, F32)
    l_ref[...] = jnp.zeros(l_ref.shape, F32)
    acc_ref[...] = jnp.zeros(acc_ref.shape, F32)

    if sel:
        sels = [sel_ref[e if sel == "per_head" else 0] for e in range(n_sub)]

    def raw_scores(j):
        kb = k_ref[pl.ds(pl.multiple_of(j * tk, tk), tk), :]
        return [lax.dot_general(kb, qs[e], (((1,), (1,)), ((), ())), preferred_element_type=F32)
                for e in range(n_sub)]

    def park(ss):
        for e in range(n_sub):
            s_ref[e] = ss[e]

    def chunk(j, j_next, masked):
        ss_next = raw_scores(j_next)
        ss = [s_ref[e] for e in range(n_sub)]
        off = pl.multiple_of(j * tk, tk)
        vt = vt_ref[:, pl.ds(off, tk)]
        mask = None
        if masked:
            kpos = off + lax.broadcasted_iota(jnp.int32, (tk, 1), 0)
            mask = kpos <= qpos
            if window:
                mask = jnp.logical_and(mask, qpos - kpos <= WINDOW)
        for e in range(n_sub):
            if bias:
                ss[e] = ss[e] - ck_ref[pl.ds(off, tk), e:e + 1]
            me = mask
            if sel:
                selm = jnp.dot(e_ref[pl.ds(off, tk), :], sels[e], preferred_element_type=F32) > 0.5
                me = selm if mask is None else jnp.logical_and(selm, mask)
            if me is not None:
                ss[e] = jnp.where(me, ss[e], NEG_INF)
        m_prev = [m_ref[e] for e in range(n_sub)]
        m_new, ps = [], []
        for e in range(n_sub):
            top = jnp.max(ss[e], axis=0, keepdims=True)
            if bias:
                top = top + cq_ref[e:e + 1, :]
            m_new.append(jnp.maximum(m_prev[e], top))
            shift = m_new[e] - cq_ref[e:e + 1, :] if bias else m_new[e]
            ps.append(jnp.exp2(ss[e] - shift))
        pvs = [jnp.dot(vt, ps[e].astype(BF), preferred_element_type=F32) for e in range(n_sub)]
        for e in range(n_sub):
            alpha = jnp.exp2(m_prev[e] - m_new[e])
            l_ref[e] = alpha * l_ref[e] + jnp.sum(ps[e], axis=0, keepdims=True)
            acc_ref[e] = alpha * acc_ref[e] + pvs[e]
            m_ref[e] = m_new[e]
        park(ss_next)

    if not causal:
        lo, n_open, hi = 0, n_k, n_k
    elif window:
        lo = jnp.maximum(qi * tq - WINDOW, 0) // tk
        n_open, hi = lo, ((qi + 1) * tq + tk - 1) // tk
    else:
        lo = 0
        n_open = (qi * tq) // tk
        hi = ((qi + 1) * tq + tk - 1) // tk

    def masked_body(j, carry):
        chunk(j, jnp.minimum(j + 1, hi - 1), True)
        return carry

    def open_body(j, carry):
        chunk(j, jnp.minimum(j + 1, hi - 1), False)
        return carry

    park(raw_scores(lo))
    lax.fori_loop(lo, n_open, open_body, 0)
    lax.fori_loop(n_open, hi, masked_body, 0)

    outs = []
    for e in range(n_sub):
        l = l_ref[e]
        outs.append(acc_ref[e] * jnp.where(l > 0.0, 1.0 / l, 0.0))
    if mode == "full":
        ot = outs[0]
    elif mode == "pair":
        sub = lax.broadcasted_iota(jnp.int32, (LANES, 1), 0)
        ot = jnp.where(sub < HEAD_DIM, outs[0], outs[1])
    else:
        ot = outs[0] - lam_ref[0] * outs[1]
    o = ot.T
    if mode == "diff":
        y = o * lax.rsqrt(jnp.mean(o * o, axis=-1, keepdims=True) + NORM_EPS)
        o = (y * g_ref[...]) * (1.0 - lam_init)
    o_ref[...] = o.astype(o_ref.dtype)


def flash_prompt(q, k, vt, *, n_batch, t_q, t_k, n_cols, q_col, k_col, v_col, mode, scale,
                 causal=True, window=False, bias=None, sel=None, sel_mode=None, emat=None,
                 lam=None, subln_g=None, lam_init=0.0, tq=512, tk=512):
    tq = min(tq, t_q)
    tk = min(tk, t_k)
    nq = t_q // tq
    n_k = t_k // tk
    in_specs = [
        pl.BlockSpec((tq, LANES), lambda b, c, i: (b * nq + i, q_col(c))),
        pl.BlockSpec((t_k, LANES), lambda b, c, i: (b, k_col(c))),
        pl.BlockSpec((None, LANES, t_k), lambda b, c, i: (b, v_col(c), 0)),
    ]
    args = [q, k, vt]
    if bias is not None:
        cq, ck = bias
        in_specs += [pl.BlockSpec((None, None, 2, tq), lambda b, c, i: (b, c, 0, i)),
                     pl.BlockSpec((None, None, t_k, 2), lambda b, c, i: (b, c, 0, 0))]
        args += [cq, ck]
    if sel is not None:
        if sel_mode == "per_head":
            in_specs.append(pl.BlockSpec((None, 2, LANES, tq), lambda b, c, i: (b, c, 0, i)))
        else:
            in_specs.append(pl.BlockSpec((None, 1, LANES, tq), lambda b, c, i: (b, c // 2, 0, i)))
        in_specs.append(pl.BlockSpec((t_k, LANES), lambda b, c, i: (0, 0)))
        args += [sel, emat]
    if mode == "diff":
        in_specs += [pl.BlockSpec(memory_space=pltpu.SMEM), pl.BlockSpec((1, LANES), lambda b, c, i: (0, 0))]
        args += [lam.reshape(1).astype(F32), subln_g.reshape(1, LANES).astype(F32)]
    n_sub = 1 if mode == "full" else 2
    kern = functools.partial(_flash_kernel, mode=mode, tq=tq, tk=tk, n_k=n_k, scale=scale, causal=causal,
                             window=window, bias=bias is not None, sel=sel_mode if sel is not None else None,
                             lam_init=lam_init)
    return pl.pallas_call(
        kern,
        grid=(n_batch, n_cols, nq),
        in_specs=in_specs,
        out_specs=pl.BlockSpec((tq, LANES), lambda b, c, i: (b * nq + i, c)),
        out_shape=jax.ShapeDtypeStruct((n_batch * t_q, n_cols * LANES), F32),
        scratch_shapes=[pltpu.VMEM((n_sub, 1, tq), F32), pltpu.VMEM((n_sub, 1, tq), F32),
                        pltpu.VMEM((n_sub, LANES, tq), F32), pltpu.VMEM((n_sub, tk, tq), F32)],
        compiler_params=_cparams("parallel", "parallel", "arbitrary"),
        name="flash_" + mode,
    )(*args)


def _cmp_attn_kernel(q_ref, k_ref, v_ref, ov_ref, o_ref, imp_ref, *, tq, scale):
    qi = pl.program_id(2)
    lane = lax.broadcasted_iota(jnp.int32, (1, LANES), 1)
    n_c = k_ref.shape[0]
    qpos = qi * tq + lax.broadcasted_iota(jnp.int32, (tq, 1), 0)
    ends = lax.broadcasted_iota(jnp.int32, (1, n_c), 1) * CMP_STRIDE + (CMP_LEN - 1)
    mask = ends <= qpos
    kb = k_ref[...].astype(BF)
    vb = v_ref[...].astype(BF)
    ov = ov_ref[...]
    imp = jnp.zeros((tq, LANES), F32)
    halves = []
    for pair in range(2):
        q = q_ref[:, pair * LANES:(pair + 1) * LANES].astype(F32) * scale
        outs = []
        for e in range(2):
            qe = (jnp.where(lane < HEAD_DIM, q, 0.0) if e == 0 else jnp.where(lane >= HEAD_DIM, q, 0.0)).astype(BF)
            s = lax.dot_general(qe, kb, (((1,), (1,)), ((), ())), preferred_element_type=F32)
            s = jnp.where(mask, s, NEG_INF)
            m = jnp.max(s, axis=-1, keepdims=True)
            p = jnp.where(mask, jnp.exp(s - m), 0.0)
            l = jnp.sum(p, axis=-1, keepdims=True)
            p = p * jnp.where(l > 0.0, 1.0 / l, 0.0)
            pb = p.astype(BF)
            outs.append(jnp.dot(pb, vb, preferred_element_type=F32))
            imp = imp + jnp.dot(pb, ov, preferred_element_type=F32)
        halves.append(jnp.where(lane < HEAD_DIM, outs[0], outs[1]))
    o_ref[:, 0:LANES] = halves[0]
    o_ref[:, LANES:2 * LANES] = halves[1]
    imp_ref[...] = imp


def cmp_attn_prompt(q, k_dup, v_dup, ov, *, n_batch, t_q, tq=256):
    tq = min(tq, t_q)
    nq = t_q // tq
    n_c = k_dup.shape[2]
    kern = functools.partial(_cmp_attn_kernel, tq=tq, scale=HEAD_DIM ** -0.5)
    return pl.pallas_call(
        kern,
        grid=(n_batch, G_NSA, nq),
        in_specs=[pl.BlockSpec((tq, 2 * LANES), lambda b, g, i: (b * nq + i, g)),
                  pl.BlockSpec((None, None, n_c, LANES), lambda b, g, i: (b, g, 0, 0)),
                  pl.BlockSpec((None, None, n_c, LANES), lambda b, g, i: (b, g, 0, 0)),
                  pl.BlockSpec((n_c, LANES), lambda b, g, i: (0, 0))],
        out_specs=[pl.BlockSpec((tq, 2 * LANES), lambda b, g, i: (b * nq + i, g)),
                   pl.BlockSpec((None, None, tq, LANES), lambda b, g, i: (b, g, i, 0))],
        out_shape=[jax.ShapeDtypeStruct((n_batch * t_q, G_NSA * 2 * LANES), F32),
                   jax.ShapeDtypeStruct((n_batch, G_NSA, t_q, LANES), F32)],
        compiler_params=_cparams("parallel", "parallel", "arbitrary"),
        name="cmp_attn",
    )(q, k_dup, v_dup, ov)


def _decode_kernel(*refs, has_bias, has_new, emit_p):
    it = iter(refs)
    q_ref, k_ref, v_ref = next(it), next(it), next(it)
    b_ref = next(it) if has_bias else None
    if has_new:
        kn_ref, vn_ref, bn_ref = next(it), next(it), next(it)
    o_ref = next(it)
    p_ref = next(it) if emit_p else None
    m_ref, l_ref, acc_ref = next(it), next(it), next(it)
    j = pl.program_id(1)
    last = pl.num_programs(1) - 1

    @pl.when(j == 0)
    def _():
        m_ref[...] = jnp.full(m_ref.shape, NEG_INF, F32)
        l_ref[...] = jnp.zeros(l_ref.shape, F32)
        acc_ref[...] = jnp.zeros(acc_ref.shape, F32)

    q = q_ref[...]

    def update(kb, vb, b):
        s = lax.dot_general(q, kb.astype(BF), (((1,), (1,)), ((), ())), preferred_element_type=F32)
        if b is not None:
            s = s + b
        m_prev = m_ref[...]
        m_new = jnp.maximum(m_prev, jnp.max(s, axis=-1, keepdims=True))
        alpha = jnp.exp(m_prev - m_new)
        p = jnp.exp(s - m_new)
        l_ref[...] = alpha * l_ref[...] + jnp.sum(p, axis=-1, keepdims=True)
        acc_ref[...] = alpha * acc_ref[...] + jnp.dot(p.astype(BF), vb.astype(BF), preferred_element_type=F32)
        m_ref[...] = m_new
        return p

    p = update(k_ref[...], v_ref[...], b_ref[...] if has_bias else None)

    @pl.when(j == last)
    def _():
        if has_new:
            update(kn_ref[...], vn_ref[...], bn_ref[...])
        inv = 1.0 / l_ref[...]
        o_ref[...] = acc_ref[...] * inv
        if emit_p:
            p_ref[...] = p * inv


def decode_attn(qbd, k, v, *, k_col, v_col, ck, cv, bias=None, new=None, emit_p=False, rk=1024):
    n_b, r = k.shape[0], k.shape[1]
    rk = min(rk, r)
    n_j = r // rk
    assert not emit_p or (n_j == 1 and new is None)
    in_specs = [pl.BlockSpec((None, LANES, ck), lambda b, j: (b, 0, 0)),
                pl.BlockSpec((None, rk, ck), lambda b, j: (b, j, k_col)),
                pl.BlockSpec((None, rk, cv), lambda b, j: (b, j, v_col))]
    args = [qbd, k, v]
    if bias is not None:
        in_specs.append(pl.BlockSpec((None, LANES, rk), lambda b, j: (b, 0, j)))
        args.append(bias)
    if new is not None:
        kn, vn, bn = new
        in_specs += [pl.BlockSpec((None, NEW_PAD, ck), lambda b, j: (b, 0, k_col)),
                     pl.BlockSpec((None, NEW_PAD, cv), lambda b, j: (b, 0, v_col)),
                     pl.BlockSpec((None, LANES, NEW_PAD), lambda b, j: (b, 0, 0))]
        args += [kn, vn, bn]
    out_specs = [pl.BlockSpec((None, LANES, cv), lambda b, j: (b, 0, 0))]
    out_shape = [jax.ShapeDtypeStruct((n_b, LANES, cv), F32)]
    if emit_p:
        out_specs.append(pl.BlockSpec((None, LANES, rk), lambda b, j: (b, 0, 0)))
        out_shape.append(jax.ShapeDtypeStruct((n_b, LANES, r), F32))
    kern = functools.partial(_decode_kernel, has_bias=bias is not None, has_new=new is not None, emit_p=emit_p)
    out = pl.pallas_call(
        kern,
        grid=(n_b, n_j),
        in_specs=in_specs,
        out_specs=out_specs,
        out_shape=out_shape,
        scratch_shapes=[pltpu.VMEM((LANES, 1), F32), pltpu.VMEM((LANES, 1), F32), pltpu.VMEM((LANES, cv), F32)],
        compiler_params=_cparams("parallel", "arbitrary"),
        name="decode_attn",
    )(*args)
    return out if emit_p else out[0]


def _paged_kernel(*refs, pp, n_pages, rpp, mode, blk_shift, has_new):
    it = iter(refs)
    pt_ref, q_ref = next(it), next(it)
    kv_refs = [next(it) for _ in range(pp)]
    b_refs = sel_ref = f_ref = np_ref = None
    if mode == "dense":
        b_refs = [next(it) for _ in range(pp)]
    if mode == "sel":
        sel_ref = next(it)
    if mode == "fox":
        f_ref = next(it)
        np_ref = next(it)
    if has_new:
        kn_ref, vn_ref, bn_ref = next(it), next(it), next(it)
    o_ref, m_ref, l_ref, acc_ref = next(it), next(it), next(it), next(it)
    carry_ref = next(it) if mode == "fox" else None
    step = pl.program_id(1)
    last = pl.num_programs(1) - 1

    @pl.when(step == 0)
    def _():
        m_ref[...] = jnp.full(m_ref.shape, NEG_INF, F32)
        l_ref[...] = jnp.zeros(l_ref.shape, F32)
        acc_ref[...] = jnp.zeros(acc_ref.shape, F32)
        if mode == "fox":
            carry_ref[...] = jnp.zeros(carry_ref.shape, F32)

    q = q_ref[...]
    nr = q.shape[0]
    width = pp * rpp
    kcat = jnp.concatenate([kv_refs[i][0] for i in range(pp)], axis=1).astype(BF)
    s = jnp.dot(q, kcat, preferred_element_type=F32)
    if mode == "dense":
        s = s + jnp.concatenate([b_refs[i][...] for i in range(pp)], axis=1)
    if mode == "sel":
        col = lax.broadcasted_iota(jnp.int32, (1, width), 1)
        rshift = int(math.log2(rpp))
        key = (n_pages - 1 - step * pp - jnp.right_shift(col, rshift)) * rpp + jnp.bitwise_and(col, rpp - 1)
        expand = (lax.broadcasted_iota(jnp.int32, (LANES, width), 0) == jnp.right_shift(key, blk_shift)).astype(BF)
        s = s + jnp.dot(sel_ref[...], expand, preferred_element_type=F32)
    if mode == "fox":
        n_h = f_ref.shape[1]
        lf = [f_ref[pp - 1 - i] for i in range(pp)]
        upper = (lax.broadcasted_iota(jnp.int32, (rpp, rpp), 0) >
                 lax.broadcasted_iota(jnp.int32, (rpp, rpp), 1)).astype(BF)
        lf_all = jnp.concatenate(lf + lf if pp * n_h < 16 else lf, axis=0)
        hi = lf_all.astype(BF)
        lo = (lf_all - hi.astype(F32)).astype(BF)
        suffix = jnp.dot(hi, upper, preferred_element_type=F32) + jnp.dot(lo, upper, preferred_element_type=F32)
        carry = carry_ref[...]
        parts = []
        for i in range(pp):
            parts.append(suffix[i * n_h:(i + 1) * n_h] + carry)
            carry = carry + jnp.sum(lf[i], axis=-1, keepdims=True)
        carry_ref[...] = carry
        b8 = jnp.concatenate(parts, axis=1)
        s = s + (jnp.concatenate([b8] * (nr // n_h), axis=0) + jnp.concatenate([np_ref[...]] * pp, axis=1))

    def accumulate(s, pv_of):
        m_prev = m_ref[...]
        m_new = jnp.maximum(m_prev, jnp.max(s, axis=-1, keepdims=True))
        alpha = jnp.exp(m_prev - m_new)
        p = jnp.exp(s - m_new)
        l_ref[...] = alpha * l_ref[...] + jnp.sum(p, axis=-1, keepdims=True)
        acc_ref[...] = alpha * acc_ref[...] + pv_of(p.astype(BF))
        m_ref[...] = m_new

    vcat = jnp.concatenate([kv_refs[i][1] for i in range(pp)], axis=1).astype(BF)
    accumulate(s, lambda p: lax.dot_general(p, vcat, (((1,), (1,)), ((), ())), preferred_element_type=F32))

    @pl.when(step == last)
    def _():
        if has_new:
            sn = lax.dot_general(q, kn_ref[...].astype(BF), (((1,), (1,)), ((), ())),
                                 preferred_element_type=F32) + bn_ref[...]
            accumulate(sn, lambda p: jnp.dot(p, vn_ref[...].astype(BF), preferred_element_type=F32))
        o_ref[...] = acc_ref[...] * (1.0 / l_ref[...])


def paged_decode(pt, qbd, cache_t, *, layer, kv_pair, pp=16, mode=None, bias=None, selbias=None,
                 blk=None, logf=None, newpre=None, new=None):
    n_b, n_pages = pt.shape
    nr = qbd.shape[1]
    c, rpp = cache_t.shape[3], cache_t.shape[4]
    pp = min(pp, n_pages)
    assert n_pages % pp == 0
    n_steps = n_pages // pp

    def page_map(b, s, pt_ref, *, i):
        return (layer, pt_ref[b, n_pages - 1 - (s * pp + i)], kv_pair, 0, 0)

    in_specs = [pl.BlockSpec((None, nr, c), lambda b, s, pt_ref: (b, 0, 0))]
    args = [qbd]
    for i in range(pp):
        in_specs.append(pl.BlockSpec((None, None, 2, c, rpp), functools.partial(page_map, i=i)))
        args.append(cache_t)
    if mode == "dense":
        shared = bias.shape[0] == 1
        for i in range(pp):
            in_specs.append(pl.BlockSpec(
                (None, nr, rpp),
                functools.partial(lambda b, s, pt_ref, *, i: (0 if shared else b, 0, n_pages - 1 - (s * pp + i)), i=i)))
            args.append(bias)
    if mode == "sel":
        in_specs.append(pl.BlockSpec((None, nr, LANES), lambda b, s, pt_ref: (b, 0, 0)))
        args.append(selbias)
    if mode == "fox":
        n_h = logf.shape[2]
        in_specs.append(pl.BlockSpec((None, pp, n_h, rpp), lambda b, s, pt_ref: (b, n_steps - 1 - s, 0, 0)))
        args.append(logf)
        in_specs.append(pl.BlockSpec((None, nr, LANES), lambda b, s, pt_ref: (b, 0, 0)))
        args.append(newpre)
    if new is not None:
        kn, vn, bn = new
        in_specs += [pl.BlockSpec((None, NEW_PAD, c), lambda b, s, pt_ref: (b, 0, 0)),
                     pl.BlockSpec((None, NEW_PAD, c), lambda b, s, pt_ref: (b, 0, 0)),
                     pl.BlockSpec((None, nr, NEW_PAD), lambda b, s, pt_ref: (b, 0, 0))]
        args += [kn, vn, bn]
    scratch = [pltpu.VMEM((nr, 1), F32), pltpu.VMEM((nr, 1), F32), pltpu.VMEM((nr, c), F32)]
    if mode == "fox":
        scratch.append(pltpu.VMEM((logf.shape[2], 1), F32))
    grid_spec = pltpu.PrefetchScalarGridSpec(
        num_scalar_prefetch=1,
        grid=(n_b, n_steps),
        in_specs=in_specs,
        out_specs=pl.BlockSpec((None, nr, c), lambda b, s, pt_ref: (b, 0, 0)),
        scratch_shapes=scratch,
    )
    kern = functools.partial(_paged_kernel, pp=pp, n_pages=n_pages, rpp=rpp, mode=mode,
                             blk_shift=None if blk is None else int(math.log2(blk)), has_new=new is not None)
    return pl.pallas_call(
        kern,
        grid_spec=grid_spec,
        out_shape=jax.ShapeDtypeStruct((n_b, nr, c), F32),
        compiler_params=_cparams("parallel", "arbitrary"),
        name="paged_" + (mode or "plain"),
    )(pt, *args)


def _rows_kernel(*refs, pp, v_shift, has_new):
    it = iter(refs)
    pt_ref, q_ref = next(it), next(it)
    x_refs = [next(it) for _ in range(pp)]
    b_ref = next(it)
    if has_new:
        xn_ref, bn_ref = next(it), next(it)
    o_ref, m_ref, l_ref, acc_ref = next(it), next(it), next(it), next(it)
    step = pl.program_id(1)
    last = pl.num_programs(1) - 1

    @pl.when(step == 0)
    def _():
        m_ref[...] = jnp.full(m_ref.shape, NEG_INF, F32)
        l_ref[...] = jnp.zeros(l_ref.shape, F32)
        acc_ref[...] = jnp.zeros(acc_ref.shape, F32)

    q = q_ref[...]

    def accumulate(x, bias):
        s = lax.dot_general(q, x, (((1,), (1,)), ((), ())), preferred_element_type=F32) + bias
        m_prev = m_ref[...]
        m_new = jnp.maximum(m_prev, jnp.max(s, axis=-1, keepdims=True))
        alpha = jnp.exp(m_prev - m_new)
        p = jnp.exp(s - m_new)
        l_ref[...] = alpha * l_ref[...] + jnp.sum(p, axis=-1, keepdims=True)
        pv = jnp.dot(pltpu.roll(p, v_shift, 1).astype(BF), x, preferred_element_type=F32)
        acc_ref[...] = alpha * acc_ref[...] + pv
        m_ref[...] = m_new

    accumulate(jnp.concatenate([x_refs[i][...] for i in range(pp)], axis=0).astype(BF), b_ref[...])

    @pl.when(step == last)
    def _():
        if has_new:
            accumulate(xn_ref[...].astype(BF), bn_ref[...])
        o_ref[...] = acc_ref[...] * (1.0 / l_ref[...])


def paged_rows(pt, q, cache_rows, bias, *, layer, v_shift, pp=8, new=None):
    n_b, n_pages = pt.shape
    nr = q.shape[1]
    rows = cache_rows.shape[2]
    pp = min(pp, n_pages)
    assert n_pages % pp == 0
    in_specs = [pl.BlockSpec((None, nr, LANES), lambda b, s, pt_ref: (b, 0, 0))]
    args = [q]
    for i in range(pp):
        in_specs.append(pl.BlockSpec(
            (None, None, rows, LANES),
            functools.partial(lambda b, s, pt_ref, *, i: (layer, pt_ref[b, s * pp + i], 0, 0), i=i)))
        args.append(cache_rows)
    in_specs.append(pl.BlockSpec((nr, pp * rows), lambda b, s, pt_ref: (0, 0)))
    args.append(jnp.concatenate([bias] * pp, axis=1))
    if new is not None:
        in_specs += [pl.BlockSpec((None, NEW_PAD, LANES), lambda b, s, pt_ref: (b, 0, 0)),
                     pl.BlockSpec((nr, NEW_PAD), lambda b, s, pt_ref: (0, 0))]
        args += list(new)
    grid_spec = pltpu.PrefetchScalarGridSpec(
        num_scalar_prefetch=1,
        grid=(n_b, n_pages // pp),
        in_specs=in_specs,
        out_specs=pl.BlockSpec((None, nr, LANES), lambda b, s, pt_ref: (b, 0, 0)),
        scratch_shapes=[pltpu.VMEM((nr, 1), F32), pltpu.VMEM((nr, 1), F32), pltpu.VMEM((nr, LANES), F32)],
    )
    return pl.pallas_call(
        functools.partial(_rows_kernel, pp=pp, v_shift=v_shift, has_new=new is not None),
        grid_spec=grid_spec,
        out_shape=jax.ShapeDtypeStruct((n_b, nr, LANES), F32),
        compiler_params=_cparams("parallel", "arbitrary"),
        name="paged_rows",
    )(pt, *args)


def _block_sum_kernel(*refs, pp, ppb):
    pt_ref = refs[0]
    k_refs = refs[1:1 + pp]
    o_ref = refs[1 + pp]
    ones = jnp.ones((8, k_refs[0].shape[1]), BF)
    rows = []
    for bi in range(pp // ppb):
        acc = None
        for j in range(ppb):
            x = k_refs[bi * ppb + j][...]
            hi = x.astype(BF)
            lo = (x - hi.astype(F32)).astype(BF)
            t = (lax.dot_general(ones, hi, (((1,), (1,)), ((), ())), preferred_element_type=F32) +
                 lax.dot_general(ones, lo, (((1,), (1,)), ((), ())), preferred_element_type=F32))
            acc = t if acc is None else acc + t
        rows.append(acc[0:1])
    o_ref[...] = jnp.concatenate(rows, axis=0)


def page_block_sums(pt, cache_t, *, layer, slot, ppb, pp=16):
    n_b, n_pages = pt.shape
    c, rpp = cache_t.shape[3], cache_t.shape[4]
    pp = min(pp, n_pages)
    assert n_pages % pp == 0 and pp % ppb == 0
    in_specs = []
    for i in range(pp):
        in_specs.append(pl.BlockSpec(
            (None, None, None, c, rpp),
            functools.partial(lambda b, s, pt_ref, *, i: (layer, pt_ref[b, s * pp + i], slot, 0, 0), i=i)))
    grid_spec = pltpu.PrefetchScalarGridSpec(
        num_scalar_prefetch=1,
        grid=(n_b, n_pages // pp),
        in_specs=in_specs,
        out_specs=pl.BlockSpec((None, pp // ppb, c), lambda b, s, pt_ref: (b, s, 0)),
    )
    return pl.pallas_call(
        functools.partial(_block_sum_kernel, pp=pp, ppb=ppb),
        grid_spec=grid_spec,
        out_shape=jax.ShapeDtypeStruct((n_b, n_pages // ppb, c), F32),
        compiler_params=_cparams("parallel", "arbitrary"),
        name="page_block_sums",
    )(pt, *([cache_t] * pp))


def _merge_kernel(a_ref, b_ref, c_ref, d_ref, w_ref, g0_ref, g1_ref, g2_ref, g3_ref, o_ref):
    acc = None
    for n, (br, g) in enumerate(((a_ref, g0_ref), (b_ref, g1_ref), (c_ref, g2_ref), (d_ref, g3_ref))):
        y = jnp.dot(br[...].astype(BF), w_ref[n].astype(BF), preferred_element_type=F32)
        t = jax.nn.sigmoid(g[...].astype(F32)) * y
        acc = t if acc is None else acc + t
    o_ref[...] = acc.astype(o_ref.dtype)


def merge_branches(branches, w_branch, glog, d_model, layer):
    n, bw = branches[0].shape
    tm = _tile(n, 640)
    tn = _tile(d_model, 512, LANES)
    nj = d_model // tn
    in_specs = [pl.BlockSpec((tm, bw), lambda i, j: (i, 0)) for _ in range(N_BRANCH)]
    in_specs.append(pl.BlockSpec((None, N_BRANCH, bw, tn), lambda i, j: (layer, 0, 0, j)))
    for nb in range(N_BRANCH):
        in_specs.append(pl.BlockSpec((tm, tn), functools.partial(lambda i, j, nb: (i, nb * nj + j), nb=nb)))
    return pl.pallas_call(
        _merge_kernel,
        grid=(n // tm, nj),
        in_specs=in_specs,
        out_specs=pl.BlockSpec((tm, tn), lambda i, j: (i, j)),
        out_shape=jax.ShapeDtypeStruct((n, d_model), BF),
        compiler_params=_cparams("parallel", "arbitrary"),
        name="merge",
    )(*branches, w_branch, glog, glog, glog, glog)


def _moe_kernel(be_ref, x_ref, wg_ref, wu_ref, wd_ref, o_ref):
    x = x_ref[...].astype(BF)
    g = jnp.dot(x, wg_ref[...].astype(BF), preferred_element_type=F32)
    u = jnp.dot(x, wu_ref[...].astype(BF), preferred_element_type=F32)
    hid = (jax.nn.silu(g) * u).astype(BF)
    o_ref[...] = jnp.dot(hid, wd_ref[...].astype(BF), preferred_element_type=F32)


def moe_experts(xs, blk_e, w_g, w_u, w_d, layer):
    rows, d = xs.shape
    f = w_g.shape[3]
    n_blk = rows // MOE_BLOCK
    grid_spec = pltpu.PrefetchScalarGridSpec(
        num_scalar_prefetch=1,
        grid=(n_blk,),
        in_specs=[pl.BlockSpec((MOE_BLOCK, d), lambda i, be: (i, 0)),
                  pl.BlockSpec((None, None, d, f), lambda i, be: (layer, be[i], 0, 0)),
                  pl.BlockSpec((None, None, d, f), lambda i, be: (layer, be[i], 0, 0)),
                  pl.BlockSpec((None, None, f, d), lambda i, be: (layer, be[i], 0, 0))],
        out_specs=pl.BlockSpec((MOE_BLOCK, d), lambda i, be: (i, 0)),
    )
    return pl.pallas_call(
        _moe_kernel,
        grid_spec=grid_spec,
        out_shape=jax.ShapeDtypeStruct((rows, d), F32),
        compiler_params=_cparams("arbitrary"),
        name="moe_experts",
    )(blk_e, xs, w_g, w_u, w_d)


def _rope_tables(pos):
    half = ROPE_DIMS // 2
    inv_freq = jnp.exp(jnp.arange(half, dtype=F32) * (-2.0 * math.log(ROPE_THETA) / ROPE_DIMS))
    ang = jnp.asarray(pos, F32)[:, None] * inv_freq[None, :]
    return jnp.cos(ang), jnp.sin(ang)


def _rope(x, cos, sin):
    half = ROPE_DIMS // 2
    c = cos[:, None, :]
    s = sin[:, None, :]
    x1, x2 = x[..., :half], x[..., half:ROPE_DIMS]
    return jnp.concatenate([x1 * c - x2 * s, x2 * c + x1 * s, x[..., ROPE_DIMS:]], axis=-1)


def _overlap_matrix(n_cmp, n_slc):
    st = np.arange(n_cmp) * CMP_STRIDE
    en = st + CMP_LEN
    j = np.arange(n_slc) * SEL_BLOCK
    return ((st[:, None] < j[None, :] + SEL_BLOCK) & (en[:, None] > j[None, :])).astype(np.float32)


def _block_expand_t(n_keys, block):
    return ((np.arange(n_keys) // block)[:, None] == np.arange(LANES)[None, :]).astype(np.float32)


def _top_mask(score, k):
    n = score.shape[-1]
    idx = jnp.arange(n, dtype=jnp.int32)
    mine = score[..., :, None]
    other = score[..., None, :]
    ahead = (other > mine) | ((other == mine) & (idx[None, :] < idx[:, None]))
    return jnp.sum(ahead.astype(jnp.int32), axis=-1) < k


def _nsa_select(imp, q_pos, n_slc):
    own = q_pos // SEL_BLOCK
    j = jnp.arange(n_slc, dtype=jnp.int32)[None, :]
    valid = j <= own[:, None]
    forced = (j == 0) | (j == own[:, None]) | (j == own[:, None] - 1)
    score = jnp.where(valid, imp + jnp.where(forced, SEL_BONUS, 0.0), -SEL_BONUS)
    return _top_mask(score, min(N_SEL, n_slc)) & valid[None, None]


def _moba_select(sc, q_pos, n_blk):
    own = q_pos // MOBA_BLOCK
    past_blk = jnp.arange(n_blk, dtype=jnp.int32)[None, :] < own[:, None]
    sc = jnp.where(past_blk[None, None], sc, NEG_INF)
    return _top_mask(sc, min(MOBA_TOPK, n_blk)) & past_blk[None, None]


def _pad_lanes(x, n=LANES):
    return jnp.pad(x, [(0, 0)] * (x.ndim - 1) + [(0, n - x.shape[-1])])


def _pad_rows(x, n=LANES):
    return jnp.pad(x, [(0, 0)] * (x.ndim - 2) + [(0, n - x.shape[-2]), (0, 0)])


def _compress(k, pe, w1, w2):
    b, tk, g, d = k.shape
    n_cmp = (tk - CMP_LEN) // CMP_STRIDE + 1
    idx = np.arange(n_cmp)[:, None] * CMP_STRIDE + np.arange(CMP_LEN)[None, :]
    blk = k[:, idx] + pe[None, None, :, None, :]
    blk = jnp.transpose(blk, (0, 1, 3, 2, 4)).reshape(b * n_cmp * g, CMP_LEN * d)
    return cmp_mlp(blk, w1, w2).reshape(b, n_cmp, g, d)


def _key_minor(cache):
    l, p, r, a, b, d = cache.shape
    return jnp.transpose(cache, (0, 1, 3, 4, 5, 2)).reshape(l, p, a, b * d, r)


def _qbd(q, n_heads, d_head, scale):
    b, tq, c = q.shape
    qh = q.reshape(b, tq, n_heads, d_head) * scale
    eye = jnp.eye(n_heads, dtype=q.dtype)
    bd = jnp.einsum("bthd,hg->bthgd", qh, eye).reshape(b, tq * n_heads, c)
    return _pad_rows(bd, DEC_ROWS).astype(BF)


def _diag_blocks(o, n_heads, tq, d_out):
    b = o.shape[0]
    o = o[:, :tq * n_heads].reshape(b, tq, n_heads, n_heads, d_out)
    idx = jnp.arange(n_heads)
    return o[:, :, idx, idx]


def _group_blocks(o, tq, hpg):
    b = o.shape[0]
    o = o[:, :tq * G_NSA * hpg].reshape(b, tq, G_NSA, hpg, G_NSA, HEAD_DIM)
    gi = jnp.arange(G_NSA)
    o = o[:, :, gi, :, gi]
    return jnp.transpose(o, (1, 2, 0, 3, 4)).reshape(b, tq, G_NSA * hpg * HEAD_DIM)


def _query_rows(x):
    b, t, h, r = x.shape
    return _pad_rows(x.reshape(b, t * h, r), DEC_ROWS)


def kernel(x_prompt, x_sample, cache_diff_kv, cache_fox_kv, cache_fox_logf, cache_nsa_kv, cache_nsa_win,
           cache_moba_kv, cache_mem_kv, page_table, mem_prompt, norm_mix_g, w_in, b_forget, diff_lambda,
           diff_subln_g, nsa_cmp_pe, nsa_cmp_w1, nsa_cmp_w2, w_branch, w_out, norm_mem_g, norm_memkv_g,
           w_mem_q, w_mem_kv, w_mem_o, norm_ffn_g, w_router_group, b_router_group, w_router_expert,
           b_router_expert, w_exp_gate, w_exp_up, w_exp_down, norm_final_g):
    depth = w_in.shape[0]
    bp, t, d_model = x_prompt.shape
    bs, ts, _ = x_sample.shape
    n_pool, page = cache_diff_kv.shape[1], cache_diff_kv.shape[2]
    n_pages = page_table.shape[1]
    past_len = n_pages * page
    bw = d_model // N_BRANCH
    h_diff = bw // (2 * HEAD_DIM)
    h8 = bw // HEAD_DIM
    hpg = h8 // G_NSA
    gw = G_NSA * HEAD_DIM
    n_mem = mem_prompt.shape[1]
    mem_hd = bw // H_MEM
    n_exp = N_GROUPS * EXPERTS_PER_GROUP
    np_tok = bp * t
    ns_tok = bs * ts
    n_tok = np_tok + ns_tok
    assert h8 * ts <= DEC_ROWS and ts < CMP_STRIDE and past_len % MOBA_BLOCK == 0 and page % CMP_STRIDE == 0
    assert past_len // SEL_BLOCK <= LANES and DEC_ROWS % h8 == 0 and 2 * h_diff == h8 and page == LANES
    scale = HEAD_DIM ** -0.5
    page_table = page_table.astype(jnp.int32)

    pos_p = np.arange(t, dtype=np.int32)
    pos_s = past_len + np.arange(ts, dtype=np.int32)
    pos_all = np.concatenate([np.tile(pos_p, bp), np.tile(pos_s, bs)])
    cos, sin = _rope_tables(pos_all)

    sizes = (bw, bw, bw, bw, bw, bw, h8, bw, 6 * G_NSA * HEAD_DIM, 3 * h8, bw, bw, bw, N_BRANCH * d_model)
    offs = np.concatenate([[0], np.cumsum(sizes)])
    seg = lambda i: (int(offs[i]), int(offs[i + 1]))
    main_ids = (0, 1, 2, 3, 4, 5, 7, 8, 10, 11, 12)
    small_ids = (6, 9)
    n_main = sum(sizes[i] for i in main_ids)
    n_small = sum(sizes[i] for i in small_ids)
    n_main_pad = -(-(n_main + n_small) // 512) * 512
    mo = {}
    o = 0
    for i in main_ids + small_ids:
        mo[i] = o
        o += sizes[i]

    x = jnp.concatenate([x_prompt.reshape(np_tok, d_model), x_sample.reshape(ns_tok, d_model)], axis=0)

    e_moba = jnp.asarray(_block_expand_t(t, MOBA_BLOCK), BF)
    e_nsa = jnp.asarray(_block_expand_t(t, SEL_BLOCK), BF)
    n_cmp_p = (t - CMP_LEN) // CMP_STRIDE + 1
    n_cmp_p_pad = -(-n_cmp_p // LANES) * LANES
    n_slc_p = -(-t // SEL_BLOCK)
    ov_p = np.zeros((n_cmp_p_pad, LANES), np.float32)
    ov_p[:n_cmp_p, :n_slc_p] = _overlap_matrix(n_cmp_p, n_slc_p)
    ov_p = jnp.asarray(ov_p, BF)
    n_mblk_p = -(-t // MOBA_BLOCK)

    tk_s = past_len + ts
    n_cmp_s = (tk_s - CMP_LEN) // CMP_STRIDE + 1
    n_cmp_s_pad = -(-n_cmp_s // LANES) * LANES
    n_slc_s = -(-tk_s // SEL_BLOCK)
    n_slc_past = past_len // SEL_BLOCK
    ov_s = jnp.asarray(_overlap_matrix(n_cmp_s, n_slc_s))
    n_mblk_s = -(-tk_s // MOBA_BLOCK)
    n_mblk_past = past_len // MOBA_BLOCK
    def row_query(n_rows, n_heads):
        return np.minimum(np.arange(n_rows) // n_heads, ts - 1)

    row_t = row_query(DEC_ROWS, h8)
    used = np.arange(DEC_ROWS) < h8 * ts

    def new_bias(n_heads):
        keyi = np.arange(NEW_PAD)
        okm = (keyi[None, :] <= row_query(DEC_ROWS, n_heads)[:, None]) & (keyi[None, :] < ts)
        return np.where(okm, 0.0, NEG_INF).astype(np.float32)

    def new_rows(a):
        return _pad_rows(a.reshape(bs, ts, a.shape[-1]), NEW_PAD)

    fox_t = _key_minor(cache_fox_kv)
    moba_t = _key_minor(cache_moba_kv)
    nsa_t = _key_minor(cache_nsa_kv)
    win_t = _key_minor(cache_nsa_win)
    logf_t = jnp.transpose(cache_fox_logf, (0, 1, 3, 2))
    diff_rows = cache_diff_kv.reshape(depth, n_pool, page * 2 * h_diff, 2 * HEAD_DIM)
    mem_rows = cache_mem_kv.reshape(depth, bs, n_mem * 2 * H_MEM, mem_hd)
    slot_d = np.arange(page * 2 * h_diff) % (2 * h_diff)
    head_d = (np.arange(DEC_ROWS) % h8) // 2
    diff_bias = jnp.asarray(np.where(slot_d[None, :] == head_d[:, None], 0.0, NEG_INF).astype(np.float32))
    slot_n = np.arange(NEW_PAD) % (2 * h_diff)
    tok_n = np.arange(NEW_PAD) // (2 * h_diff)
    diff_ok_n = ((slot_n[None, :] == head_d[:, None]) & (tok_n[None, :] <= row_query(DEC_ROWS, h8)[:, None])
                 & (tok_n[None, :] < ts))
    diff_bias_new = jnp.asarray(np.where(diff_ok_n, 0.0, NEG_INF).astype(np.float32))
    slot_m = np.arange(n_mem * 2 * H_MEM) % (2 * H_MEM)
    mem_bias = jnp.asarray(np.where(slot_m[None, :] == (np.arange(DEC_ROWS) % H_MEM)[:, None], 0.0,
                                    NEG_INF).astype(np.float32))
    comp_eye = jnp.eye(2, dtype=F32)
    wb = cache_nsa_win.shape[2]
    own_pt = jnp.arange(bs, dtype=jnp.int32).reshape(bs, 1)
    cpp = page // CMP_STRIDE

    win_ok = (np.arange(wb)[None, :] >= row_t[:, None] + (wb - WINDOW)) | ~used[:, None]
    win_bias = jnp.asarray(np.where(win_ok, 0.0, NEG_INF).astype(np.float32)).reshape(1, DEC_ROWS, wb)
    ends_s = np.arange(n_cmp_s_pad) * CMP_STRIDE + (CMP_LEN - 1)
    cmp_ok = ((ends_s[None, :] <= (past_len + row_query(LANES, h8))[:, None])
              & (np.arange(n_cmp_s_pad)[None, :] < n_cmp_s))
    cmp_bias = jnp.broadcast_to(jnp.asarray(np.where(cmp_ok, 0.0, NEG_INF).astype(np.float32)),
                                (bs, LANES, n_cmp_s_pad))

    acc_p = [[] for _ in range(7)]
    acc_s = [[] for _ in range(6)]

    for l in range(depth):
        lam_init = 0.8 - 0.6 * math.exp(-0.3 * l)
        lam_p = diff_lambda[l].astype(F32)
        lam = jnp.exp(jnp.sum(lam_p[0] * lam_p[1])) - jnp.exp(jnp.sum(lam_p[2] * lam_p[3])) + lam_init

        wl = w_in[l]
        w_main = jnp.concatenate([wl[:, seg(i)[0]:seg(i)[1]] for i in main_ids + small_ids], axis=1)
        w_main = jnp.pad(w_main, ((0, 0), (0, n_main_pad - n_main - n_small))).astype(BF)
        w_gate = wl[:, seg(13)[0]:seg(13)[1]].astype(BF)
        h = rmsnorm(x, norm_mix_g[l], BF)
        proj = matmul(h, w_main)
        glog = matmul(h, w_gate, tn_target=1024)

        def col(i):
            return proj[:, mo[i]:mo[i] + sizes[i]]

        dq = _rope(col(0).reshape(n_tok, 2 * h_diff, HEAD_DIM), cos, sin).reshape(n_tok, bw)
        dk = _rope(col(1).reshape(n_tok, 2 * h_diff, HEAD_DIM), cos, sin).reshape(n_tok, bw)
        diff_new = jnp.concatenate([dk, col(2)], axis=1)
        fq = col(3)
        fox_new = jnp.concatenate([col(4), col(5)], axis=1)
        logf_new = jax.nn.log_sigmoid(col(6).astype(F32) + b_forget[l].astype(F32))
        nq = _rope(col(7).reshape(n_tok, h8, HEAD_DIM), cos, sin).reshape(n_tok, bw)
        nkv = col(8).reshape(n_tok, 6, G_NSA, HEAD_DIM)
        nsa_new = jnp.stack([nkv[:, 0], nkv[:, 1], _rope(nkv[:, 2], cos, sin), nkv[:, 3]], axis=1)
        win_new = jnp.stack([_rope(nkv[:, 4], cos, sin), nkv[:, 5]], axis=1)
        ngate = jax.nn.sigmoid(col(9).astype(F32)).reshape(n_tok, h8, 3)
        mq = _rope(col(10).reshape(n_tok, h8, HEAD_DIM), cos, sin).reshape(n_tok, bw)
        mk = _rope(col(11).reshape(n_tok, h8, HEAD_DIM), cos, sin).reshape(n_tok, bw)
        moba_new = jnp.concatenate([mk, col(12)], axis=1)
        nsa_flat = nsa_new.reshape(n_tok, 4 * gw)
        win_flat = win_new.reshape(n_tok, 2 * gw)

        P = slice(0, np_tok)

        def keys_bf(a):
            return a.astype(BF)

        def vals_t(a):
            return jnp.transpose(a.reshape(bp, t, a.shape[-1]), (0, 2, 1)).astype(BF)

        o_a_p = flash_prompt(dq[P], keys_bf(dk[P]), vals_t(col(2)[P]), n_batch=bp, t_q=t, t_k=t, n_cols=h_diff,
                             q_col=lambda c: c, k_col=lambda c: c, v_col=lambda c: c, mode="diff",
                             scale=scale, lam=lam, subln_g=diff_subln_g[l], lam_init=lam_init)
        logf_p = logf_new[P].reshape(bp, t, h8)
        c_all = (jnp.cumsum(logf_p, axis=1) * LOG2E).reshape(bp, t, h8 // 2, 2)
        cq = jnp.transpose(c_all, (0, 2, 3, 1))
        ck = jnp.transpose(c_all, (0, 2, 1, 3))
        o_b_p = flash_prompt(fq[P], keys_bf(col(4)[P]), vals_t(col(5)[P]), n_batch=bp, t_q=t, t_k=t, n_cols=h8 // 2,
                             q_col=lambda c: c, k_col=lambda c: c, v_col=lambda c: c, mode="pair",
                             scale=scale, bias=(cq, ck))
        nsa_p = nsa_new[P].reshape(bp, t, 4, G_NSA, HEAD_DIM)
        k_cmp = _compress(nsa_p[:, :, 0], nsa_cmp_pe[l, 0], nsa_cmp_w1[l, 0], nsa_cmp_w2[l, 0])
        v_cmp = _compress(nsa_p[:, :, 1], nsa_cmp_pe[l, 1], nsa_cmp_w1[l, 1], nsa_cmp_w2[l, 1])

        def dup_cmp(a):
            a = jnp.transpose(a, (0, 2, 1, 3))
            a = jnp.concatenate([a, a], axis=-1)
            return jnp.pad(a, ((0, 0), (0, 0), (0, n_cmp_p_pad - n_cmp_p), (0, 0)))

        o_cmp_p, imp_p = cmp_attn_prompt(nq[P], dup_cmp(k_cmp), dup_cmp(v_cmp), ov_p, n_batch=bp, t_q=t)
        sel_nsa = _nsa_select(imp_p[..., :n_slc_p], jnp.asarray(pos_p), n_slc_p)
        sel_nsa = _pad_rows(jnp.transpose(sel_nsa, (0, 1, 3, 2)).astype(BF))

        def dup_g(a):
            return jnp.concatenate([a, a], axis=-1).reshape(a.shape[0], G_NSA * LANES)

        nsa_pt = nsa_new[P]
        ppg = hpg // 2
        o_sel_p = flash_prompt(nq[P], keys_bf(dup_g(nsa_pt[:, 2])), vals_t(dup_g(nsa_pt[:, 3])), n_batch=bp, t_q=t,
                               t_k=t, n_cols=h8 // 2, q_col=lambda c: c, k_col=lambda c: c // ppg,
                               v_col=lambda c: c // ppg, mode="pair", scale=scale, sel=sel_nsa,
                               sel_mode="per_group", emat=e_nsa)
        win_p = win_new[P]
        o_win_p = flash_prompt(nq[P], keys_bf(dup_g(win_p[:, 0])), vals_t(dup_g(win_p[:, 1])), n_batch=bp, t_q=t,
                               t_k=t, n_cols=h8 // 2, q_col=lambda c: c, k_col=lambda c: c // ppg,
                               v_col=lambda c: c // ppg, mode="pair", scale=scale, window=True)
        mk_p = mk[P].reshape(bp, n_mblk_p, MOBA_BLOCK, h8, HEAD_DIM)
        k_mean_p = jnp.transpose(jnp.mean(mk_p.astype(F32), axis=2), (0, 2, 1, 3))
        sc_p = jnp.einsum("bqhd,bhnd->bhqn", mq[P].reshape(bp, t, h8, HEAD_DIM).astype(F32), k_mean_p)
        sel_moba = _moba_select(sc_p, jnp.asarray(pos_p), n_mblk_p)
        own_p = jnp.asarray(pos_p // MOBA_BLOCK)
        sel_moba = sel_moba | (jnp.arange(n_mblk_p)[None, :] == own_p[:, None])[None, None]
        sel_moba = _pad_rows(jnp.transpose(sel_moba, (0, 1, 3, 2)).astype(BF))
        o_d_p = flash_prompt(mq[P], keys_bf(mk[P]), vals_t(col(12)[P]), n_batch=bp, t_q=t, t_k=t, n_cols=h8 // 2,
                             q_col=lambda c: c, k_col=lambda c: c, v_col=lambda c: c, mode="pair",
                             scale=scale, sel=sel_moba, sel_mode="per_head", emat=e_moba)

        S = slice(np_tok, n_tok)
        nb8 = jnp.broadcast_to(jnp.asarray(new_bias(h8)), (bs, DEC_ROWS, NEW_PAD))
        dq5 = dq[S].reshape(bs, ts, h_diff, 2, HEAD_DIM) * scale
        q_rows = jnp.einsum("bthcd,ce->bthced", dq5, comp_eye).reshape(bs, ts * h8, 2 * HEAD_DIM)
        q_rows = _pad_rows(q_rows, DEC_ROWS).astype(BF)
        x_new = _pad_rows(diff_new[S].reshape(bs, ts * 2 * h_diff, 2 * HEAD_DIM), NEW_PAD)
        o = paged_rows(page_table, q_rows, diff_rows, diff_bias, layer=l, v_shift=h_diff,
                       new=(x_new, diff_bias_new))
        o = o[:, :ts * h8].reshape(bs, ts, h_diff, 2, 2 * HEAD_DIM)
        o = o[:, :, :, 0] - lam * o[:, :, :, 1]
        y = o * lax.rsqrt(jnp.mean(o * o, axis=-1, keepdims=True) + NORM_EPS)
        o_a_s = ((y * diff_subln_g[l].astype(F32)) * (1.0 - lam_init)).reshape(ns_tok, bw)

        logf_s = logf_new[S].reshape(bs, ts, h8)
        pre_new = jnp.cumsum(logf_s, axis=1)
        newpre = jnp.broadcast_to(_query_rows(pre_new[..., None]), (bs, DEC_ROWS, LANES))
        d_new = pre_new[:, :, None, :] - pre_new[:, None, :, :]
        d_new = _pad_lanes(_query_rows(jnp.transpose(d_new, (0, 1, 3, 2))), NEW_PAD)
        fnew = new_rows(fox_new[S])
        qbd = _qbd(fq[S].reshape(bs, ts, bw), h8, HEAD_DIM, scale)
        o = paged_decode(page_table, qbd, fox_t, layer=l, kv_pair=0, mode="fox", logf=logf_t[l][page_table],
                         newpre=newpre, new=(fnew[..., :bw], fnew[..., bw:], d_new + nb8))
        o_b_s = _diag_blocks(o, h8, ts, HEAD_DIM).reshape(ns_tok, bw)

        kv_cmp_s = []
        hid = nsa_cmp_w1.shape[3]
        w_rows = jnp.transpose(nsa_cmp_w1[l].reshape(2, 2, CMP_STRIDE, HEAD_DIM, hid), (0, 2, 3, 1, 4))
        w_rows = w_rows.reshape(2, CMP_STRIDE, HEAD_DIM, 2 * hid)
        w_chunk = jnp.einsum("wrdc,gk->wrgdkc", w_rows, jnp.eye(G_NSA, dtype=F32))
        w_chunk = w_chunk.reshape(2, CMP_STRIDE, gw, G_NSA * 2 * hid).astype(BF)
        ab_all = nsa_chunk_proj(nsa_t, w_chunk, layer=l)
        for w in range(2):
            w1 = nsa_cmp_w1[l, w]
            ab = ab_all[w]
            ab = ab.reshape(n_pool, cpp, G_NSA, 2 * hid)[page_table].reshape(bs, n_pages * cpp, G_NSA, 2 * hid)
            pre = ab[:, :n_cmp_s, :, :hid] + ab[:, 1:n_cmp_s + 1, :, hid:]
            kv = cmp_finish(pre.reshape(bs * n_cmp_s * G_NSA, hid), nsa_cmp_pe[l, w].reshape(-1), w1,
                            nsa_cmp_w2[l, w])
            kv_cmp_s.append(kv.reshape(bs, n_cmp_s, gw))
        kv_cmp_s = jnp.pad(jnp.concatenate(kv_cmp_s, axis=-1), ((0, 0), (0, n_cmp_s_pad - n_cmp_s), (0, 0)))
        qbd_n = _qbd(nq[S].reshape(bs, ts, bw), h8, HEAD_DIM, scale)
        qbd_g = qbd_n.reshape(bs, DEC_ROWS, G_NSA, hpg, HEAD_DIM).sum(axis=3).reshape(bs, DEC_ROWS, gw)
        o, p_c = decode_attn(_pad_rows(qbd_g, LANES), kv_cmp_s, kv_cmp_s, k_col=0, v_col=1, ck=gw, cv=gw,
                             bias=cmp_bias, emit_p=True, rk=n_cmp_s_pad)
        o_cmp_s = _group_blocks(o, ts, hpg).reshape(ns_tok, bw)
        p_c = p_c[:, :h8 * ts, :n_cmp_s].reshape(bs, ts, G_NSA, hpg, n_cmp_s)
        imp_s = jnp.einsum("bqgmn,ns->bgqs", p_c, ov_s)
        sel_s = _nsa_select(imp_s, jnp.asarray(pos_s), n_slc_s)
        sel_rows = jnp.broadcast_to(jnp.transpose(sel_s, (0, 2, 1, 3))[:, :, :, None, :n_slc_past],
                                    (bs, ts, G_NSA, hpg, n_slc_past)).reshape(bs, ts, h8, n_slc_past)
        selbias = _pad_lanes(jnp.where(_query_rows(sel_rows.astype(F32)) > 0.5, 0.0, NEG_INF))
        selbias = jnp.where(jnp.asarray(used)[None, :, None], selbias, 0.0).astype(BF)
        nnew = new_rows(nsa_flat[S])
        o = paged_decode(page_table, qbd_g, nsa_t, layer=l, kv_pair=1, pp=32, mode="sel", selbias=selbias,
                         blk=SEL_BLOCK, new=(nnew[..., 2 * gw:3 * gw], nnew[..., 3 * gw:], nb8))
        o_sel_s = _group_blocks(o, ts, hpg).reshape(ns_tok, bw)
        wnew = new_rows(win_flat[S])
        o = paged_decode(own_pt, qbd_g, win_t, layer=l, kv_pair=0, mode="dense", bias=win_bias,
                         new=(wnew[..., :gw], wnew[..., gw:], nb8))
        o_win_s = _group_blocks(o, ts, hpg).reshape(ns_tok, bw)
        wbuf = cache_nsa_win[l].reshape(bs, wb, 2 * gw)
        win_state = jnp.concatenate([wbuf, win_flat[S].reshape(bs, ts, 2 * gw)], axis=1)[:, ts:]

        k_sum = page_block_sums(page_table, moba_t, layer=l, slot=0, ppb=MOBA_BLOCK // page)
        k_mean_s = (k_sum / MOBA_BLOCK).reshape(bs, n_mblk_past, h8, HEAD_DIM)
        k_mean_s = jnp.pad(jnp.transpose(k_mean_s, (0, 2, 1, 3)), ((0, 0), (0, 0), (0, n_mblk_s - n_mblk_past), (0, 0)))
        sc_s = jnp.einsum("bqhd,bhnd->bhqn", mq[S].reshape(bs, ts, h8, HEAD_DIM).astype(F32), k_mean_s)
        selm_s = _moba_select(sc_s, jnp.asarray(pos_s), n_mblk_s)
        selm_rows = _query_rows(jnp.transpose(selm_s[..., :n_mblk_past], (0, 2, 1, 3)).astype(F32))
        mbias = _pad_lanes(jnp.where(selm_rows > 0.5, 0.0, NEG_INF))
        mbias = jnp.where(jnp.asarray(used)[None, :, None], mbias, 0.0).astype(BF)
        mnew = new_rows(moba_new[S])
        qbd = _qbd(mq[S].reshape(bs, ts, bw), h8, HEAD_DIM, scale)
        o = paged_decode(page_table, qbd, moba_t, layer=l, kv_pair=0, mode="sel", selbias=mbias,
                         blk=MOBA_BLOCK, new=(mnew[..., :bw], mnew[..., bw:], nb8))
        o_d_s = _diag_blocks(o, h8, ts, HEAD_DIM).reshape(ns_tok, bw)

        o_a = jnp.concatenate([o_a_p, o_a_s], axis=0)
        o_b = jnp.concatenate([o_b_p, o_b_s], axis=0)
        o_cmp = jnp.concatenate([o_cmp_p, o_cmp_s], axis=0).reshape(n_tok, h8, HEAD_DIM)
        o_sel = jnp.concatenate([o_sel_p, o_sel_s], axis=0).reshape(n_tok, h8, HEAD_DIM)
        o_win = jnp.concatenate([o_win_p, o_win_s], axis=0).reshape(n_tok, h8, HEAD_DIM)
        o_c = (ngate[..., 0:1] * o_cmp + ngate[..., 1:2] * o_sel + ngate[..., 2:3] * o_win).reshape(n_tok, bw)
        o_d = jnp.concatenate([o_d_p, o_d_s], axis=0)
        merged = merge_branches([o_a, o_b, o_c, o_d], w_branch, glog, d_model, l)
        x = matmul(merged, w_out, res=x, layer=l)

        mem_h = rmsnorm(mem_prompt.reshape(bp * n_mem, d_model), norm_memkv_g[l], BF)
        mem_kv_p = matmul(mem_h, w_mem_kv, layer=l)
        hm = rmsnorm(x, norm_mem_g[l], BF)
        qm = matmul(hm, w_mem_q, layer=l)
        mem_vt = jnp.transpose(mem_kv_p[:, bw:].reshape(bp, n_mem, bw), (0, 2, 1)).astype(BF)
        o_m_p = flash_prompt(qm[P], mem_kv_p[:, :bw].astype(BF), mem_vt, n_batch=bp, t_q=t, t_k=n_mem,
                             n_cols=H_MEM, q_col=lambda c: c, k_col=lambda c: c, v_col=lambda c: c, mode="full",
                             scale=mem_hd ** -0.5, causal=False)
        qm_rows = _pad_rows(qm[S].reshape(bs, ts * H_MEM, mem_hd) * mem_hd ** -0.5, DEC_ROWS).astype(BF)
        o = paged_rows(own_pt, qm_rows, mem_rows, mem_bias, layer=l, v_shift=H_MEM)
        o_m_s = o[:, :ts * H_MEM].reshape(ns_tok, bw)
        x = matmul(jnp.concatenate([o_m_p, o_m_s], axis=0), w_mem_o, res=x, layer=l)

        hf = rmsnorm(x, norm_ffn_g[l], BF)
        w_r = _pad_lanes(jnp.concatenate([w_router_group[l], w_router_expert[l]], axis=1))
        logits = matmul(hf, w_r, tn_target=LANES)
        rows = jnp.arange(n_tok)
        lg = logits[:, :N_GROUPS] + b_router_group[l].astype(F32)
        grp = jnp.argmax(lg, axis=-1).astype(jnp.int32)
        p_grp = jax.nn.softmax(lg, axis=-1)[rows, grp]
        le = (logits[:, N_GROUPS:N_GROUPS + n_exp] + b_router_expert[l].astype(F32)).reshape(n_tok, N_GROUPS, EXPERTS_PER_GROUP)
        p_exp = jax.nn.softmax(le[rows, grp], axis=-1)
        w_top, i_top, p_left = [], [], p_exp
        for _ in range(TOPK_IN_GROUP):
            i_k = jnp.argmax(p_left, axis=-1).astype(jnp.int32)
            w_top.append(jnp.max(p_left, axis=-1))
            i_top.append(i_k)
            p_left = jnp.where(jnp.arange(EXPERTS_PER_GROUP, dtype=jnp.int32)[None, :] == i_k[:, None], -1.0, p_left)
        w_top, i_top = jnp.stack(w_top, axis=-1), jnp.stack(i_top, axis=-1)
        gate = p_grp[:, None] * w_top / jnp.sum(w_top, axis=-1, keepdims=True)
        eid = grp[:, None] * EXPERTS_PER_GROUP + i_top.astype(jnp.int32)
        kk = TOPK_IN_GROUP
        m_asg = n_tok * kk
        flat_e = eid.reshape(m_asg)
        onehot = (flat_e[:, None] == jnp.arange(n_exp, dtype=jnp.int32)[None, :]).astype(F32)
        cb = _tile(m_asg, 128)
        oh3 = onehot.reshape(m_asg // cb, cb, n_exp)
        tri = jnp.asarray(np.tril(np.ones((cb, cb), np.float32), -1))
        within = jnp.einsum("ij,bjk->bik", tri, oh3)
        blk_tot = jnp.sum(oh3, axis=1)
        blk_off = jnp.cumsum(blk_tot, axis=0) - blk_tot
        counts = jnp.sum(blk_tot, axis=0).astype(jnp.int32)
        padded = ((counts + MOE_BLOCK - 1) // MOE_BLOCK) * MOE_BLOCK
        pad_end = jnp.cumsum(padded)
        pad_start = pad_end - padded
        slot = jnp.sum((within + blk_off[:, None, :] + pad_start.astype(F32)[None, None, :]) * oh3, axis=-1)
        dest_flat = slot.reshape(m_asg).astype(jnp.int32)
        n_blk = -(-(m_asg + n_exp * (MOE_BLOCK - 1)) // MOE_BLOCK)
        slot_tok = jnp.zeros((n_blk * MOE_BLOCK,), jnp.int32).at[dest_flat].set(
            jnp.arange(m_asg, dtype=jnp.int32) // kk)
        xs = hf[slot_tok]
        blk_first = jnp.arange(n_blk, dtype=jnp.int32) * MOE_BLOCK
        blk_e = jnp.minimum(jnp.sum((pad_end[None, :] <= blk_first[:, None]).astype(jnp.int32), axis=1),
                            n_exp - 1).astype(jnp.int32)
        ys = moe_experts(xs, blk_e, w_exp_gate, w_exp_up, w_exp_down, l)
        contrib = ys[dest_flat].reshape(n_tok, kk, d_model) * gate[..., None].astype(F32)
        x = x + jnp.sum(contrib, axis=1)

        def pr(a, shape):
            return a[P].reshape((bp, t) + shape)

        def sr(a, shape):
            return a[S].reshape((bs, ts) + shape)

        acc_p[0].append(pr(diff_new, (2, h_diff, 2 * HEAD_DIM)))
        acc_p[1].append(pr(fox_new, (2, h8, HEAD_DIM)))
        acc_p[2].append(pr(logf_new, (h8,)))
        acc_p[3].append(pr(nsa_flat, (4, G_NSA, HEAD_DIM)))
        acc_p[4].append(pr(win_flat, (2, G_NSA, HEAD_DIM))[:, t - min(WINDOW, t):])
        acc_p[5].append(pr(moba_new, (2, h8, HEAD_DIM)))
        acc_p[6].append(mem_kv_p.reshape(bp, n_mem, 2, H_MEM, mem_hd))
        acc_s[0].append(sr(diff_new, (2, h_diff, 2 * HEAD_DIM)))
        acc_s[1].append(sr(fox_new, (2, h8, HEAD_DIM)))
        acc_s[2].append(sr(logf_new, (h8,)))
        acc_s[3].append(sr(nsa_flat, (4, G_NSA, HEAD_DIM)))
        acc_s[4].append(win_state.reshape(bs, wb, 2, G_NSA, HEAD_DIM))
        acc_s[5].append(sr(moba_new, (2, h8, HEAD_DIM)))

    y = rmsnorm(x, norm_final_g, F32)
    y_prompt = y[:np_tok].reshape(bp, t, d_model)
    y_sample = y[np_tok:].reshape(bs, ts, d_model)
    return (y_prompt, y_sample) + tuple(jnp.stack(a) for a in acc_p) + tuple(jnp.stack(a) for a in acc_s)
```
